```python
import jax, jax.numpy as jnp
from jax import lax
import numpy as np

D_MODEL = 1024
BATCH = 4
SEQ = 4096
DEPTH = 2

PLE_DIM = 256
POOL_WINDOWS = (2, 4, 8, 16)
N_POOL_GROUPS = 4
POOL_WIDTH = D_MODEL
POOL_GROUP_DIM = POOL_WIDTH // N_POOL_GROUPS
SGU_WIDTH = D_MODEL
SGU_CHUNK = 128
SGU_GROUPS = 8
SGU_GROUP_DIM = SGU_WIDTH // SGU_GROUPS
N_BRANCH = 2
IN_WIDTH = POOL_WIDTH + 2 * SGU_WIDTH + N_BRANCH * D_MODEL
D_FF = 4 * D_MODEL
EPS = 1e-6

kernel_name = "hybrid_pool_sgu_gated_block"


def rmsnorm(x, g):
    xf = x.astype(jnp.float32)
    y = xf * lax.rsqrt(jnp.mean(xf * xf, axis=-1, keepdims=True) + EPS)
    return (y * g.astype(jnp.float32)).astype(x.dtype)


def layernorm(x, g, b):
    xf = x.astype(jnp.float32)
    mu = jnp.mean(xf, axis=-1, keepdims=True)
    xc = xf - mu
    y = xc * lax.rsqrt(jnp.mean(xc * xc, axis=-1, keepdims=True) + EPS)
    return (y * g.astype(jnp.float32) + b.astype(jnp.float32)).astype(x.dtype)


def pool_mixer(z, pool_w, pool_scale):
    B, S, _ = z.shape
    zf = z.astype(jnp.float32).reshape(B, S, N_POOL_GROUPS, POOL_GROUP_DIM)
    c = jnp.cumsum(zf, axis=1)
    t = jnp.arange(S)
    outs = []
    for gi, w in enumerate(POOL_WINDOWS):
        cg = c[:, :, gi]
        lagged = jnp.pad(cg, ((0, 0), (w, 0), (0, 0)))[:, :S]
        cnt = jnp.minimum(t + 1, w).astype(jnp.float32)[None, :, None]
        outs.append((cg - lagged) / cnt)
    pooled = (jnp.stack(outs, axis=2) - zf).astype(z.dtype)
    mixed = jnp.einsum('bsgc,gcd->bsgd', pooled, pool_w)
    return mixed.reshape(B, S, POOL_WIDTH) * pool_scale


def sgu_mixer(u, v, ln_g, ln_b, w_s, b_s):
    B, S, _ = v.shape
    n_chunks = S // SGU_CHUNK
    vn = layernorm(v, ln_g, ln_b).reshape(B, n_chunks, SGU_CHUNK, SGU_GROUPS, SGU_GROUP_DIM)
    causal = jnp.tril(jnp.ones((SGU_CHUNK, SGU_CHUNK), dtype=bool))
    ws = jnp.where(causal[None], w_s, jnp.zeros_like(w_s))
    mixed = jnp.einsum('hts,bcshd->bcthd', ws, vn) + b_s.T[None, None, :, :, None]
    return u * mixed.reshape(B, S, SGU_WIDTH)


def setup_inputs(seed: int = 0) -> dict:
    key = jax.random.key(seed)
    ks = jax.random.split(key, 24)

    def nrm(k, shape, scale):
        return jax.random.normal(k, shape, jnp.float32) * scale

    def gain(k, shape):
        return 1.0 + 0.05 * jax.random.normal(k, shape, jnp.float32)

    L, D = DEPTH, D_MODEL
    return {
        "x": nrm(ks[0], (BATCH, SEQ, D), 1.0),
        "p": nrm(ks[1], (DEPTH, BATCH, SEQ, PLE_DIM), 1.0),
        "pre_mix_g": gain(ks[2], (L, D)),
        "w_in": nrm(ks[3], (L, D, IN_WIDTH), D ** -0.5),
        "b_in": nrm(ks[4], (L, IN_WIDTH), 0.02),
        "pool_w": nrm(ks[5], (L, N_POOL_GROUPS, POOL_GROUP_DIM, POOL_GROUP_DIM), POOL_GROUP_DIM ** -0.5),
        "pool_scale": gain(ks[6], (L, POOL_WIDTH)),
        "sgu_ln_g": gain(ks[7], (L, SGU_WIDTH)),
        "sgu_ln_b": nrm(ks[8], (L, SGU_WIDTH), 0.02),
        "sgu_w_s": nrm(ks[9], (L, SGU_GROUPS, SGU_CHUNK, SGU_CHUNK), SGU_CHUNK ** -0.5),
        "sgu_b_s": 1.0 + 0.1 * jax.random.normal(ks[10], (L, SGU_GROUPS, SGU_CHUNK), jnp.float32),
        "w_pa": nrm(ks[11], (L, POOL_WIDTH, D), POOL_WIDTH ** -0.5),
        "w_pb": nrm(ks[12], (L, SGU_WIDTH, D), SGU_WIDTH ** -0.5),
        "w_o": nrm(ks[13], (L, D, D), D ** -0.5),
        "post_mix_g": gain(ks[14], (L, D)),
        "pre_ffn_g": gain(ks[15], (L, D)),
        "w_ff1": nrm(ks[16], (L, D, D_FF), D ** -0.5),
        "w_ff2": nrm(ks[17], (L, D_FF, D), D_FF ** -0.5),
        "post_ffn_g": gain(ks[18], (L, D)),
        "w_ple_gate": nrm(ks[19], (L, D, D), D ** -0.5),
        "w_ple_proj": nrm(ks[20], (L, PLE_DIM, D), PLE_DIM ** -0.5),
        "post_ple_g": gain(ks[21], (L, D)),
    }


def reference(x, p, pre_mix_g, w_in, b_in, pool_w, pool_scale, sgu_ln_g, sgu_ln_b, sgu_w_s,
              sgu_b_s, w_pa, w_pb, w_o, post_mix_g, pre_ffn_g, w_ff1, w_ff2, post_ffn_g,
              w_ple_gate, w_ple_proj, post_ple_g):
    B, S, D = x.shape
    for i in range(DEPTH):
        h = rmsnorm(x, pre_mix_g[i])
        proj = h @ w_in[i] + b_in[i]
        z = proj[..., :POOL_WIDTH]
        uv = jax.nn.gelu(proj[..., POOL_WIDTH:POOL_WIDTH + 2 * SGU_WIDTH])
        gates = jax.nn.sigmoid(proj[..., POOL_WIDTH + 2 * SGU_WIDTH:]).reshape(B, S, N_BRANCH, D)
        u, v = uv[..., :SGU_WIDTH], uv[..., SGU_WIDTH:]
        ya = pool_mixer(z, pool_w[i], pool_scale[i]) @ w_pa[i]
        yb = sgu_mixer(u, v, sgu_ln_g[i], sgu_ln_b[i], sgu_w_s[i], sgu_b_s[i]) @ w_pb[i]
        merged = gates[:, :, 0] * ya + gates[:, :, 1] * yb
        x = x + rmsnorm(merged @ w_o[i], post_mix_g[i])
        h = rmsnorm(x, pre_ffn_g[i])
        f = jnp.square(jax.nn.relu(h @ w_ff1[i])) @ w_ff2[i]
        x = x + rmsnorm(f, post_ffn_g[i])
        gate = jax.nn.sigmoid(x @ w_ple_gate[i])
        e = p[i] @ w_ple_proj[i]
        x = x + rmsnorm(gate * e, post_ple_g[i])
    return x
```

```python
import functools

import jax
import jax.numpy as jnp
from jax import lax
from jax.experimental import pallas as pl
from jax.experimental.pallas import tpu as pltpu

POOL_WINDOWS = (2, 4, 8, 16)
HALO = 16
SGU_CHUNK = 128
SGU_GROUPS = 8
EPS = 1e-6
ROW_TILE = 256
VMEM_LIMIT_BYTES = 56 * 1024 * 1024

_BF16 = jnp.bfloat16
_F32 = jnp.float32


def _rms(x, g):
    return x * lax.rsqrt(jnp.mean(x * x, axis=-1, keepdims=True) + EPS) * g


def _dot(a, b):
    return jnp.dot(a, b, preferred_element_type=_F32)


def _mixer_kernel(x_ref, g_pre_ref, w_in_ref, b_in_ref, pool_w_ref, pool_scale_ref,
                  ln_g_ref, ln_b_ref, w_s_ref, b_s_ref, w_pa_ref, w_pb_ref, w_o_ref,
                  g_post_ref, o_ref, zbuf_ref, *, tiles_per_seq):
    tm, d = x_ref.shape
    j = pl.program_id(0) % tiles_per_seq

    @pl.when(j == 0)
    def _():
        zbuf_ref[0:HALO, :] = jnp.zeros((HALO, d), _F32)

    @pl.when(j != 0)
    def _():
        zbuf_ref[0:HALO, :] = zbuf_ref[tm:tm + HALO, :]

    x = x_ref[...]
    h = _rms(x, g_pre_ref[...]).astype(_BF16)

    def proj(lo, hi):
        return _dot(h, w_in_ref[:, lo:hi]) + b_in_ref[:, lo:hi]

    zbuf_ref[HALO:HALO + tm, :] = proj(0, d)
    t = j * tm + lax.broadcasted_iota(jnp.int32, (tm, 1), 0)
    gdim = d // len(POOL_WINDOWS)
    mixed_groups = []
    for k, w in enumerate(POOL_WINDOWS):
        cols = slice(k * gdim, (k + 1) * gdim)
        zk = zbuf_ref[HALO:HALO + tm, cols]
        win = zk
        for s in range(1, w):
            win = win + zbuf_ref[HALO - s:HALO - s + tm, cols]
        cnt = jnp.minimum(t + 1, w).astype(_F32)
        pooled = (win / cnt - zk).astype(_BF16)
        mixed_groups.append(_dot(pooled, pool_w_ref[k]))
    pm = (jnp.concatenate(mixed_groups, axis=-1) * pool_scale_ref[...]).astype(_BF16)
    ya = _dot(pm, w_pa_ref[...])
    merged = jax.nn.sigmoid(proj(3 * d, 4 * d)) * ya

    v = jax.nn.gelu(proj(2 * d, 3 * d))
    mu = jnp.mean(v, axis=-1, keepdims=True)
    vc = v - mu
    vn = (vc * lax.rsqrt(jnp.mean(vc * vc, axis=-1, keepdims=True) + EPS) * ln_g_ref[...]
          + ln_b_ref[...]).astype(_BF16)
    hd = d // SGU_GROUPS
    causal = (lax.broadcasted_iota(jnp.int32, (SGU_CHUNK, SGU_CHUNK), 0)
              >= lax.broadcasted_iota(jnp.int32, (SGU_CHUNK, SGU_CHUNK), 1))
    rows = []
    for c in range(tm // SGU_CHUNK):
        r = slice(c * SGU_CHUNK, (c + 1) * SGU_CHUNK)
        blocks = []
        for g in range(SGU_GROUPS):
            ws = jnp.where(causal, w_s_ref[g], jnp.zeros((), _BF16))
            blocks.append(_dot(ws, vn[r, g * hd:(g + 1) * hd]))
        rows.append(jnp.concatenate(blocks, axis=-1) + b_s_ref[...])
    spatial = jnp.concatenate(rows, axis=0)
    u = jax.nn.gelu(proj(d, 2 * d))
    yb = _dot((u * spatial).astype(_BF16), w_pb_ref[...])
    merged = merged + jax.nn.sigmoid(proj(4 * d, 5 * d)) * yb

    o = _dot(merged.astype(_BF16), w_o_ref[...])
    o_ref[...] = x + _rms(o, g_post_ref[...])


def _ffn_kernel(x_ref, p_ref, g_pre_ref, w1_ref, w2_ref, g_post_ref, wg_ref, wp_ref,
                g_ple_ref, o_ref, *, ff_chunk):
    x = x_ref[...]
    h = _rms(x, g_pre_ref[...]).astype(_BF16)
    d_ff = w1_ref.shape[1]
    f = None
    for c in range(d_ff // ff_chunk):
        a = _dot(h, w1_ref[:, c * ff_chunk:(c + 1) * ff_chunk])
        a = jnp.square(jnp.maximum(a, 0.0)).astype(_BF16)
        part = _dot(a, w2_ref[c * ff_chunk:(c + 1) * ff_chunk, :])
        f = part if f is None else f + part
    x = x + _rms(f, g_post_ref[...])
    gate = jax.nn.sigmoid(_dot(x.astype(_BF16), wg_ref[...]))
    e = _dot(p_ref[...].astype(_BF16), wp_ref[...])
    o_ref[...] = x + _rms(gate * e, g_ple_ref[...])


def _resident(shape):
    nd = len(shape)
    return pl.BlockSpec(shape, lambda i: (0,) * nd, pipeline_mode=pl.Buffered(1))


def _mixer_call(x, g_pre, w_in, b_in, pool_w, pool_scale, ln_g, ln_b, w_s, b_s_tile,
                w_pa, w_pb, w_o, g_post, *, seq):
    n, d = x.shape
    tm = ROW_TILE
    row = pl.BlockSpec((tm, d), lambda i: (i, 0))
    params = (g_pre, w_in, b_in, pool_w, pool_scale, ln_g, ln_b, w_s, b_s_tile,
              w_pa, w_pb, w_o, g_post)
    return pl.pallas_call(
        functools.partial(_mixer_kernel, tiles_per_seq=seq // tm),
        out_shape=jax.ShapeDtypeStruct((n, d), _F32),
        grid=(n // tm,),
        in_specs=[row] + [_resident(a.shape) for a in params],
        out_specs=row,
        scratch_shapes=[pltpu.VMEM((HALO + tm, d), _F32)],
        compiler_params=pltpu.CompilerParams(
            dimension_semantics=("arbitrary",), vmem_limit_bytes=VMEM_LIMIT_BYTES),
        name="mixer",
    )(x, *params)


def _ffn_call(x, p, g_pre, w1, w2, g_post, wg, wp, g_ple):
    n, d = x.shape
    tm = ROW_TILE
    row = pl.BlockSpec((tm, d), lambda i: (i, 0))
    prow = pl.BlockSpec((tm, p.shape[1]), lambda i: (i, 0))
    params = (g_pre, w1, w2, g_post, wg, wp, g_ple)
    return pl.pallas_call(
        functools.partial(_ffn_kernel, ff_chunk=1024),
        out_shape=jax.ShapeDtypeStruct((n, d), _F32),
        grid=(n // tm,),
        in_specs=[row, prow] + [_resident(a.shape) for a in params],
        out_specs=row,
        compiler_params=pltpu.CompilerParams(
            dimension_semantics=("parallel",), vmem_limit_bytes=VMEM_LIMIT_BYTES),
        name="ffn_ple",
    )(x, p, *params)


def kernel(x, p, pre_mix_g, w_in, b_in, pool_w, pool_scale, sgu_ln_g, sgu_ln_b, sgu_w_s,
           sgu_b_s, w_pa, w_pb, w_o, post_mix_g, pre_ffn_g, w_ff1, w_ff2, post_ffn_g,
           w_ple_gate, w_ple_proj, post_ple_g):
    b, s, d = x.shape
    depth = w_in.shape[0]
    assert s % ROW_TILE == 0 and ROW_TILE % SGU_CHUNK == 0 and d % SGU_GROUPS == 0
    xf = x.reshape(b * s, d)
    pf = p.reshape(depth, b * s, p.shape[-1])
    hd = d // SGU_GROUPS

    def vec(a):
        return a.reshape(1, -1)

    for i in range(depth):
        b_s_tile = jnp.repeat(sgu_b_s[i].T, hd, axis=1)
        xf = _mixer_call(
            xf, vec(pre_mix_g[i]), w_in[i].astype(_BF16), vec(b_in[i]),
            pool_w[i].astype(_BF16), vec(pool_scale[i]), vec(sgu_ln_g[i]), vec(sgu_ln_b[i]),
            sgu_w_s[i].astype(_BF16), b_s_tile, w_pa[i].astype(_BF16), w_pb[i].astype(_BF16),
            w_o[i].astype(_BF16), vec(post_mix_g[i]), seq=s)
        xf = _ffn_call(
            xf, pf[i], vec(pre_ffn_g[i]), w_ff1[i].astype(_BF16), w_ff2[i].astype(_BF16),
            vec(post_ffn_g[i]), w_ple_gate[i].astype(_BF16), w_ple_proj[i].astype(_BF16),
            vec(post_ple_g[i]))
    return xf.reshape(b, s, d)
```

```python
import functools

import jax
import jax.numpy as jnp
from jax import lax
from jax.experimental import pallas as pl
from jax.experimental.pallas import tpu as pltpu

POOL_WINDOWS = (2, 4, 8, 16)
SUBLANES = 8
HALO = 16
SGU_CHUNK = 128
SGU_GROUPS = 8
EPS = 1e-6
SUB_TILE = 256
ROW_TILE = 512
VMEM_LIMIT_BYTES = 56 * 1024 * 1024

_BF16 = jnp.bfloat16
_F32 = jnp.float32


def _rms(x, g):
    return x * lax.rsqrt(jnp.mean(x * x, axis=-1, keepdims=True) + EPS) * g


def _dot(a, b):
    return jnp.dot(a, b, preferred_element_type=_F32)


def _run_staggered(programs, lag):
    live = list(enumerate(programs))
    step = 0
    while live:
        for k, prog in list(live):
            if step >= k * lag and next(prog, StopIteration) is StopIteration:
                live.remove((k, prog))
        step += 1


def _window_sums(zext, w):
    s = zext
    k = 1
    while k < min(w, SUBLANES):
        s = s + pltpu.roll(s, k, axis=0)
        k *= 2
    win = s[HALO:, :]
    if w > SUBLANES:
        assert w == 2 * SUBLANES
        win = win + s[HALO - SUBLANES:-SUBLANES, :]
    return win


def _mixer_program(row0, seq_row0, x_ref, g_pre_ref, w_in_ref, b_in_ref, pool_w_ref,
                   pool_scale_ref, ln_g_ref, ln_b_ref, w_s_ref, b_s_ref, w_pa_ref, w_pb_ref,
                   w_o_ref, g_post_ref, o_ref, zbuf_ref):
    tm = SUB_TILE
    d = x_ref.shape[1]
    rows = slice(row0, row0 + tm)
    zrows = slice(HALO + row0, HALO + row0 + tm)

    x = x_ref[rows, :]
    h = _rms(x, g_pre_ref[...]).astype(_BF16)
    yield

    def proj(lo, hi):
        return _dot(h, w_in_ref[:, lo:hi]) + b_in_ref[:, lo:hi]

    zbuf_ref[zrows, :] = proj(0, d)
    yield
    v = jax.nn.gelu(proj(2 * d, 3 * d))
    yield

    t = seq_row0 + row0 + lax.broadcasted_iota(jnp.int32, (tm, 128), 0)
    gdim = d // len(POOL_WINDOWS)
    pooled = []
    for k, w in enumerate(POOL_WINDOWS):
        zext = zbuf_ref[row0:row0 + HALO + tm, k * gdim:(k + 1) * gdim]
        win = _window_sums(zext, w)
        inv_cnt = 1.0 / jnp.minimum(t + 1, w).astype(_F32)
        inv_cnt = jnp.concatenate([inv_cnt] * (gdim // 128), axis=1)
        pooled.append((win * inv_cnt - zext[HALO:, :]).astype(_BF16))
    yield
    gate_a = jax.nn.sigmoid(proj(3 * d, 4 * d))
    yield
    pm = jnp.concatenate([_dot(pooled[k], pool_w_ref[k]) for k in range(len(POOL_WINDOWS))], axis=-1)
    pm = (pm * pool_scale_ref[...]).astype(_BF16)
    merged = gate_a * _dot(pm, w_pa_ref[...])
    yield

    mu = jnp.mean(v, axis=-1, keepdims=True)
    vc = v - mu
    vn = (vc * lax.rsqrt(jnp.mean(vc * vc, axis=-1, keepdims=True) + EPS) * ln_g_ref[...]
          + ln_b_ref[...]).astype(_BF16)
    yield
    u = jax.nn.gelu(proj(d, 2 * d))
    yield
    hd = d // SGU_GROUPS
    n_chunks = tm // SGU_CHUNK
    causal = (lax.broadcasted_iota(jnp.int32, (SGU_CHUNK, SGU_CHUNK), 0)
              >= lax.broadcasted_iota(jnp.int32, (SGU_CHUNK, SGU_CHUNK), 1))
    blocks = [[] for _ in range(n_chunks)]
    for g in range(SGU_GROUPS):
        ws = jnp.where(causal, w_s_ref[g], jnp.zeros((), _BF16))
        rhs = jnp.concatenate([vn[c * SGU_CHUNK:(c + 1) * SGU_CHUNK, g * hd:(g + 1) * hd]
                               for c in range(n_chunks)], axis=1)
        out = _dot(ws, rhs)
        for c in range(n_chunks):
            blocks[c].append(out[:, c * hd:(c + 1) * hd])
    spatial = jnp.concatenate([jnp.concatenate(b, axis=1) + b_s_ref[...] for b in blocks], axis=0)
    yield
    gate_b = jax.nn.sigmoid(proj(4 * d, 5 * d))
    yield
    merged = merged + gate_b * _dot((u * spatial).astype(_BF16), w_pb_ref[...])
    yield
    o = _dot(merged.astype(_BF16), w_o_ref[...])
    yield
    o_ref[rows, :] = x + _rms(o, g_post_ref[...])


def _mixer_kernel(x_ref, *refs, tiles_per_seq, lag):
    zbuf_ref = refs[-1]
    tm, d = x_ref.shape
    j = pl.program_id(0) % tiles_per_seq

    @pl.when(j == 0)
    def _():
        zbuf_ref[0:HALO, :] = jnp.zeros((HALO, d), _F32)

    @pl.when(j != 0)
    def _():
        zbuf_ref[0:HALO, :] = zbuf_ref[tm:tm + HALO, :]

    _run_staggered([_mixer_program(r, j * tm, x_ref, *refs) for r in range(0, tm, SUB_TILE)], lag)


def _ffn_program(row0, x_ref, p_ref, g_pre_ref, w1_ref, w2_ref, g_post_ref, wg_ref, wp_ref,
                 g_ple_ref, o_ref, *, ff_chunk):
    rows = slice(row0, row0 + SUB_TILE)
    x = x_ref[rows, :]
    h = _rms(x, g_pre_ref[...]).astype(_BF16)
    yield
    f = None
    for c in range(w1_ref.shape[1] // ff_chunk):
        a = _dot(h, w1_ref[:, c * ff_chunk:(c + 1) * ff_chunk])
        a = jnp.square(jnp.maximum(a, 0.0)).astype(_BF16)
        part = _dot(a, w2_ref[c * ff_chunk:(c + 1) * ff_chunk, :])
        f = part if f is None else f + part
        yield
    x = x + _rms(f, g_post_ref[...])
    yield
    gate = jax.nn.sigmoid(_dot(x.astype(_BF16), wg_ref[...]))
    e = _dot(p_ref[rows, :].astype(_BF16), wp_ref[...])
    yield
    o_ref[rows, :] = x + _rms(gate * e, g_ple_ref[...])


def _ffn_kernel(x_ref, *refs, ff_chunk, lag):
    _run_staggered([_ffn_program(r, x_ref, *refs, ff_chunk=ff_chunk)
                    for r in range(0, x_ref.shape[0], SUB_TILE)], lag)


def _resident(shape):
    nd = len(shape)
    return pl.BlockSpec(shape, lambda i: (0,) * nd, pipeline_mode=pl.Buffered(1))


def _mixer_call(x, g_pre, w_in, b_in, pool_w, pool_scale, ln_g, ln_b, w_s, b_s_tile,
                w_pa, w_pb, w_o, g_post, *, seq):
    n, d = x.shape
    tm = ROW_TILE
    row = pl.BlockSpec((tm, d), lambda i: (i, 0))
    params = (g_pre, w_in, b_in, pool_w, pool_scale, ln_g, ln_b, w_s, b_s_tile,
              w_pa, w_pb, w_o, g_post)
    return pl.pallas_call(
        functools.partial(_mixer_kernel, tiles_per_seq=seq // tm, lag=6),
        out_shape=jax.ShapeDtypeStruct((n, d), _F32),
        grid=(n // tm,),
        in_specs=[row] + [_resident(a.shape) for a in params],
        out_specs=row,
        scratch_shapes=[pltpu.VMEM((HALO + tm, d), _F32)],
        compiler_params=pltpu.CompilerParams(
            dimension_semantics=("arbitrary",), vmem_limit_bytes=VMEM_LIMIT_BYTES),
        name="mixer",
    )(x, *params)


def _ffn_call(x, p, g_pre, w1, w2, g_post, wg, wp, g_ple):
    n, d = x.shape
    tm = ROW_TILE
    row = pl.BlockSpec((tm, d), lambda i: (i, 0))
    prow = pl.BlockSpec((tm, p.shape[1]), lambda i: (i, 0))
    params = (g_pre, w1, w2, g_post, wg, wp, g_ple)
    return pl.pallas_call(
        functools.partial(_ffn_kernel, ff_chunk=1024, lag=3),
        out_shape=jax.ShapeDtypeStruct((n, d), _F32),
        grid=(n // tm,),
        in_specs=[row, prow] + [_resident(a.shape) for a in params],
        out_specs=row,
        compiler_params=pltpu.CompilerParams(
            dimension_semantics=("parallel",), vmem_limit_bytes=VMEM_LIMIT_BYTES),
        name="ffn_ple",
    )(x, p, *params)


def kernel(x, p, pre_mix_g, w_in, b_in, pool_w, pool_scale, sgu_ln_g, sgu_ln_b, sgu_w_s,
           sgu_b_s, w_pa, w_pb, w_o, post_mix_g, pre_ffn_g, w_ff1, w_ff2, post_ffn_g,
           w_ple_gate, w_ple_proj, post_ple_g):
    b, s, d = x.shape
    depth = w_in.shape[0]
    assert s % ROW_TILE == 0 and ROW_TILE % SUB_TILE == 0 and SUB_TILE % SGU_CHUNK == 0
    assert d % SGU_GROUPS == 0 and (d // len(POOL_WINDOWS)) % 128 == 0
    xf = x.reshape(b * s, d)
    pf = p.reshape(depth, b * s, p.shape[-1])
    hd = d // SGU_GROUPS

    def vec(a):
        return a.reshape(1, -1)

    for i in range(depth):
        b_s_tile = jnp.repeat(sgu_b_s[i].T, hd, axis=1)
        xf = _mixer_call(
            xf, vec(pre_mix_g[i]), w_in[i].astype(_BF16), vec(b_in[i]),
            pool_w[i].astype(_BF16), vec(pool_scale[i]), vec(sgu_ln_g[i]), vec(sgu_ln_b[i]),
            sgu_w_s[i].astype(_BF16), b_s_tile, w_pa[i].astype(_BF16), w_pb[i].astype(_BF16),
            w_o[i].astype(_BF16), vec(post_mix_g[i]), seq=s)
        xf = _ffn_call(
            xf, pf[i], vec(pre_ffn_g[i]), w_ff1[i].astype(_BF16), w_ff2[i].astype(_BF16),
            vec(post_ffn_g[i]), w_ple_gate[i].astype(_BF16), w_ple_proj[i].astype(_BF16),
            vec(post_ple_g[i]))
    return xf.reshape(b, s, d)
```

```python
import functools

import jax
import jax.numpy as jnp
from jax import lax
from jax.experimental import pallas as pl
from jax.experimental.pallas import tpu as pltpu

POOL_WINDOWS = (2, 4, 8, 16)
SUBLANES = 8
HALO = 16
SGU_CHUNK = 128
SGU_GROUPS = 8
EPS = 1e-6
SUB_TILE = 256
ROW_TILE = 512
VMEM_LIMIT_BYTES = 56 * 1024 * 1024

_BF16 = jnp.bfloat16
_F32 = jnp.float32


def _rms(x, g):
    return x * lax.rsqrt(jnp.mean(x * x, axis=-1, keepdims=True) + EPS) * g


def _dot(a, b):
    return jnp.dot(a, b, preferred_element_type=_F32)


def _wdot(a, w_packed):
    return _dot(a, pltpu.bitcast(w_packed, _BF16))


def _pack_rows(w):
    *lead, k, n = w.shape
    pairs = jnp.swapaxes(w.astype(_BF16).reshape(*lead, k // 2, 2, n), -1, -2)
    return lax.bitcast_convert_type(pairs, jnp.uint32)


_LOG2E = 1.4426950408889634
_GELU_C1 = -2.0 * 0.7978845608028654 * _LOG2E
_GELU_C2 = _GELU_C1 * 0.044715


def _sigmoid(x):
    return 1.0 / (1.0 + jnp.exp2(x * -_LOG2E))


def _gelu_tanh(x):
    return x / (1.0 + jnp.exp2(x * (_GELU_C1 + _GELU_C2 * (x * x))))


def _run_staggered(programs, lag):
    live = list(enumerate(programs))
    step = 0
    while live:
        for k, prog in list(live):
            if step >= k * lag and next(prog, StopIteration) is StopIteration:
                live.remove((k, prog))
        step += 1


def _window_sums(zext, w):
    s = zext
    k = 1
    while k < min(w, SUBLANES):
        s = s + pltpu.roll(s, k, axis=0)
        k *= 2
    win = s[HALO:, :]
    if w > SUBLANES:
        assert w == 2 * SUBLANES
        win = win + s[HALO - SUBLANES:-SUBLANES, :]
    return win


def _mixer_program(row0, seq_row0, x_ref, g_pre_ref, w_in_ref, b_in_ref, pool_w_ref,
                   pool_scale_ref, ln_g_ref, ln_b_ref, w_s_ref, b_s_ref, w_pa_ref, w_pb_ref,
                   w_o_ref, g_post_ref, o_ref, zbuf_ref):
    tm = SUB_TILE
    d = x_ref.shape[1]
    rows = slice(row0, row0 + tm)
    zrows = slice(HALO + row0, HALO + row0 + tm)

    x = x_ref[rows, :]
    h = _rms(x, g_pre_ref[...]).astype(_BF16)
    yield

    def proj(lo, hi):
        return _wdot(h, w_in_ref[:, lo:hi]) + b_in_ref[:, lo:hi]

    zbuf_ref[zrows, :] = proj(0, d)
    yield
    v = _gelu_tanh(proj(2 * d, 3 * d))
    yield

    t = seq_row0 + row0 + lax.broadcasted_iota(jnp.int32, (tm, 128), 0)
    gdim = d // len(POOL_WINDOWS)
    pooled = []
    for k, w in enumerate(POOL_WINDOWS):
        zext = zbuf_ref[row0:row0 + HALO + tm, k * gdim:(k + 1) * gdim]
        win = _window_sums(zext, w)
        inv_cnt = 1.0 / jnp.minimum(t + 1, w).astype(_F32)
        inv_cnt = jnp.concatenate([inv_cnt] * (gdim // 128), axis=1)
        pooled.append((win * inv_cnt - zext[HALO:, :]).astype(_BF16))
    yield
    gate_a = _sigmoid(proj(3 * d, 4 * d))
    yield
    pm = jnp.concatenate([_wdot(pooled[k], pool_w_ref[k]) for k in range(len(POOL_WINDOWS))], axis=-1)
    pm = (pm * pool_scale_ref[...]).astype(_BF16)
    merged = gate_a * _wdot(pm, w_pa_ref[...])
    yield

    mu = jnp.mean(v, axis=-1, keepdims=True)
    vc = v - mu
    vn = (vc * lax.rsqrt(jnp.mean(vc * vc, axis=-1, keepdims=True) + EPS) * ln_g_ref[...]
          + ln_b_ref[...]).astype(_BF16)
    yield
    u = _gelu_tanh(proj(d, 2 * d))
    yield
    hd = d // SGU_GROUPS
    n_chunks = tm // SGU_CHUNK
    causal = (lax.broadcasted_iota(jnp.int32, (SGU_CHUNK, SGU_CHUNK), 0)
              >= lax.broadcasted_iota(jnp.int32, (SGU_CHUNK, SGU_CHUNK), 1))
    blocks = [[] for _ in range(n_chunks)]
    for g in range(SGU_GROUPS):
        ws = jnp.where(causal, w_s_ref[g], jnp.zeros((), _BF16))
        rhs = jnp.concatenate([vn[c * SGU_CHUNK:(c + 1) * SGU_CHUNK, g * hd:(g + 1) * hd]
                               for c in range(n_chunks)], axis=1)
        out = _dot(ws, rhs)
        for c in range(n_chunks):
            blocks[c].append(out[:, c * hd:(c + 1) * hd])
    spatial = jnp.concatenate([jnp.concatenate(b, axis=1) + b_s_ref[...] for b in blocks], axis=0)
    yield
    gate_b = _sigmoid(proj(4 * d, 5 * d))
    yield
    merged = merged + gate_b * _wdot((u * spatial).astype(_BF16), w_pb_ref[...])
    yield
    o = _wdot(merged.astype(_BF16), w_o_ref[...])
    yield
    o_ref[rows, :] = x + _rms(o, g_post_ref[...])


def _mixer_kernel(x_ref, *refs, tiles_per_seq, lag):
    zbuf_ref = refs[-1]
    tm, d = x_ref.shape
    j = pl.program_id(0) % tiles_per_seq

    @pl.when(j == 0)
    def _():
        zbuf_ref[0:HALO, :] = jnp.zeros((HALO, d), _F32)

    @pl.when(j != 0)
    def _():
        zbuf_ref[0:HALO, :] = zbuf_ref[tm:tm + HALO, :]

    _run_staggered([_mixer_program(r, j * tm, x_ref, *refs) for r in range(0, tm, SUB_TILE)], lag)


def _ffn_program(row0, x_ref, p_ref, g_pre_ref, w1_ref, w2_ref, g_post_ref, wg_ref, wp_ref,
                 g_ple_ref, o_ref, *, ff_chunk):
    rows = slice(row0, row0 + SUB_TILE)
    x = x_ref[rows, :]
    h = _rms(x, g_pre_ref[...]).astype(_BF16)
    yield
    f = None
    for c in range(w1_ref.shape[1] // ff_chunk):
        a = _wdot(h, w1_ref[:, c * ff_chunk:(c + 1) * ff_chunk])
        a = jnp.square(jnp.maximum(a, 0.0)).astype(_BF16)
        part = _wdot(a, w2_ref[c * ff_chunk // 2:(c + 1) * ff_chunk // 2, :])
        f = part if f is None else f + part
        yield
    x = x + _rms(f, g_post_ref[...])
    yield
    gate = _sigmoid(_wdot(x.astype(_BF16), wg_ref[...]))
    e = _wdot(p_ref[rows, :].astype(_BF16), wp_ref[...])
    yield
    o_ref[rows, :] = x + _rms(gate * e, g_ple_ref[...])


def _ffn_kernel(x_ref, *refs, ff_chunk, lag):
    _run_staggered([_ffn_program(r, x_ref, *refs, ff_chunk=ff_chunk)
                    for r in range(0, x_ref.shape[0], SUB_TILE)], lag)


def _resident(shape):
    nd = len(shape)
    return pl.BlockSpec(shape, lambda i: (0,) * nd, pipeline_mode=pl.Buffered(1))


def _mixer_call(x, g_pre, w_in, b_in, pool_w, pool_scale, ln_g, ln_b, w_s, b_s_tile,
                w_pa, w_pb, w_o, g_post, *, seq):
    n, d = x.shape
    tm = ROW_TILE
    row = pl.BlockSpec((tm, d), lambda i: (i, 0))
    params = (g_pre, w_in, b_in, pool_w, pool_scale, ln_g, ln_b, w_s, b_s_tile,
              w_pa, w_pb, w_o, g_post)
    return pl.pallas_call(
        functools.partial(_mixer_kernel, tiles_per_seq=seq // tm, lag=6),
        out_shape=jax.ShapeDtypeStruct((n, d), _F32),
        grid=(n // tm,),
        in_specs=[row] + [_resident(a.shape) for a in params],
        out_specs=row,
        scratch_shapes=[pltpu.VMEM((HALO + tm, d), _F32)],
        compiler_params=pltpu.CompilerParams(
            dimension_semantics=("arbitrary",), vmem_limit_bytes=VMEM_LIMIT_BYTES),
        name="mixer",
    )(x, *params)


def _ffn_call(x, p, g_pre, w1, w2, g_post, wg, wp, g_ple):
    n, d = x.shape
    tm = ROW_TILE
    row = pl.BlockSpec((tm, d), lambda i: (i, 0))
    prow = pl.BlockSpec((tm, p.shape[1]), lambda i: (i, 0))
    params = (g_pre, w1, w2, g_post, wg, wp, g_ple)
    return pl.pallas_call(
        functools.partial(_ffn_kernel, ff_chunk=1024, lag=3),
        out_shape=jax.ShapeDtypeStruct((n, d), _F32),
        grid=(n // tm,),
        in_specs=[row, prow] + [_resident(a.shape) for a in params],
        out_specs=row,
        compiler_params=pltpu.CompilerParams(
            dimension_semantics=("parallel",), vmem_limit_bytes=VMEM_LIMIT_BYTES),
        name="ffn_ple",
    )(x, p, *params)


def kernel(x, p, pre_mix_g, w_in, b_in, pool_w, pool_scale, sgu_ln_g, sgu_ln_b, sgu_w_s,
           sgu_b_s, w_pa, w_pb, w_o, post_mix_g, pre_ffn_g, w_ff1, w_ff2, post_ffn_g,
           w_ple_gate, w_ple_proj, post_ple_g):
    b, s, d = x.shape
    depth = w_in.shape[0]
    assert s % ROW_TILE == 0 and ROW_TILE % SUB_TILE == 0 and SUB_TILE % SGU_CHUNK == 0
    assert d % SGU_GROUPS == 0 and (d // len(POOL_WINDOWS)) % 128 == 0
    xf = x.reshape(b * s, d)
    pf = p.reshape(depth, b * s, p.shape[-1])
    hd = d // SGU_GROUPS

    def vec(a):
        return a.reshape(1, -1)

    for i in range(depth):
        b_s_tile = jnp.repeat(sgu_b_s[i].T, hd, axis=1)
        xf = _mixer_call(
            xf, vec(pre_mix_g[i]), _pack_rows(w_in[i]), vec(b_in[i]),
            _pack_rows(pool_w[i]), vec(pool_scale[i]), vec(sgu_ln_g[i]), vec(sgu_ln_b[i]),
            sgu_w_s[i].astype(_BF16), b_s_tile, _pack_rows(w_pa[i]), _pack_rows(w_pb[i]),
            _pack_rows(w_o[i]), vec(post_mix_g[i]), seq=s)
        xf = _ffn_call(
            xf, pf[i], vec(pre_ffn_g[i]), _pack_rows(w_ff1[i]), _pack_rows(w_ff2[i]),
            vec(post_ffn_g[i]), _pack_rows(w_ple_gate[i]), _pack_rows(w_ple_proj[i]),
            vec(post_ple_g[i]))
    return xf.reshape(b, s, d)
```

```python
import functools

import jax
import jax.numpy as jnp
from jax import lax
from jax.experimental import pallas as pl
from jax.experimental.pallas import tpu as pltpu

POOL_WINDOWS = (2, 4, 8, 16)
SUBLANES = 8
HALO = 16
SGU_CHUNK = 128
SGU_GROUPS = 8
EPS = 1e-6
SUB_TILE = 256
ROW_TILE = 512
STAGE_ROWS, STAGE_COLS = 256, 1024
STAGE_SLOTS = 4
VMEM_LIMIT_BYTES = 56 * 1024 * 1024

_BF16 = jnp.bfloat16
_F32 = jnp.float32
_U32 = jnp.uint32


def _rms(x, g):
    return x * lax.rsqrt(jnp.mean(x * x, axis=-1, keepdims=True) + EPS) * g


def _dot(a, b):
    return jnp.dot(a, b, preferred_element_type=_F32)


def _wdot(a, w_packed):
    return _dot(a, pltpu.bitcast(w_packed, _BF16))


_LOG2E = 1.4426950408889634
_GELU_C1 = -2.0 * 0.7978845608028654 * _LOG2E
_GELU_C2 = _GELU_C1 * 0.044715


def _sigmoid(x):
    return 1.0 / (1.0 + jnp.exp2(x * -_LOG2E))


def _gelu_tanh(x):
    return x / (1.0 + jnp.exp2(x * (_GELU_C1 + _GELU_C2 * (x * x))))


def _matrix_pieces(src, dst):
    k, n = src.shape
    pieces = []
    for r in range(0, k, STAGE_ROWS):
        nr = min(STAGE_ROWS, k - r)
        for c in range(0, n, STAGE_COLS):
            nc = min(STAGE_COLS, n - c)
            pieces.append((src.at[pl.ds(r, nr), pl.ds(c, nc)],
                           dst.at[pl.ds(r // 2, nr // 2), pl.ds(c, nc)]))
    return pieces


def _load_weights(pieces, stage_ref, sem_ref):
    slots = stage_ref.shape[0]

    def copy(i):
        src = pieces[i][0]
        nr, nc = src.shape
        return pltpu.make_async_copy(
            src, stage_ref.at[i % slots, pl.ds(0, nr), pl.ds(0, nc)], sem_ref.at[i % slots])

    for i in range(min(slots, len(pieces))):
        copy(i).start()
    for i, (src, dst) in enumerate(pieces):
        nr, nc = src.shape
        copy(i).wait()
        dst[...] = pltpu.bitcast(stage_ref[i % slots, 0:nr, 0:nc].astype(_BF16), _U32)
        if i + slots < len(pieces):
            copy(i + slots).start()


def _run_staggered(programs, lag):
    live = list(enumerate(programs))
    step = 0
    while live:
        for k, prog in list(live):
            if step >= k * lag and next(prog, StopIteration) is StopIteration:
                live.remove((k, prog))
        step += 1


def _window_sums(zext, w):
    s = zext
    k = 1
    while k < min(w, SUBLANES):
        s = s + pltpu.roll(s, k, axis=0)
        k *= 2
    win = s[HALO:, :]
    if w > SUBLANES:
        assert w == 2 * SUBLANES
        win = win + s[HALO - SUBLANES:-SUBLANES, :]
    return win


def _mixer_program(row0, seq_row0, layer, x_ref, g_pre_ref, b_in_ref, pool_scale_ref, ln_g_ref,
                   ln_b_ref, w_s_ref, b_s_ref, g_post_ref, o_ref, w_in_ref, pool_w_ref, w_pa_ref,
                   w_pb_ref, w_o_ref, zbuf_ref):
    tm = SUB_TILE
    d = x_ref.shape[1]
    rows = slice(row0, row0 + tm)
    zrows = slice(HALO + row0, HALO + row0 + tm)
    lyr = slice(layer, layer + 1)

    x = x_ref[rows, :]
    h = _rms(x, g_pre_ref[lyr, :]).astype(_BF16)
    yield

    def proj(lo, hi):
        return _wdot(h, w_in_ref[:, lo:hi]) + b_in_ref[lyr, lo:hi]

    zbuf_ref[zrows, :] = proj(0, d)
    yield
    v = _gelu_tanh(proj(2 * d, 3 * d))
    yield

    t = seq_row0 + row0 + lax.broadcasted_iota(jnp.int32, (tm, 128), 0)
    gdim = d // len(POOL_WINDOWS)
    pooled = []
    for k, w in enumerate(POOL_WINDOWS):
        zext = zbuf_ref[row0:row0 + HALO + tm, k * gdim:(k + 1) * gdim]
        win = _window_sums(zext, w)
        inv_cnt = 1.0 / jnp.minimum(t + 1, w).astype(_F32)
        inv_cnt = jnp.concatenate([inv_cnt] * (gdim // 128), axis=1)
        pooled.append((win * inv_cnt - zext[HALO:, :]).astype(_BF16))
    yield
    gate_a = _sigmoid(proj(3 * d, 4 * d))
    yield
    pm = jnp.concatenate([_wdot(pooled[k], pool_w_ref[k]) for k in range(len(POOL_WINDOWS))], axis=-1)
    pm = (pm * pool_scale_ref[lyr, :]).astype(_BF16)
    merged = gate_a * _wdot(pm, w_pa_ref[...])
    yield

    mu = jnp.mean(v, axis=-1, keepdims=True)
    vc = v - mu
    vn = (vc * lax.rsqrt(jnp.mean(vc * vc, axis=-1, keepdims=True) + EPS) * ln_g_ref[lyr, :]
          + ln_b_ref[lyr, :]).astype(_BF16)
    yield
    u = _gelu_tanh(proj(d, 2 * d))
    yield
    hd = d // SGU_GROUPS
    n_chunks = tm // SGU_CHUNK
    causal = (lax.broadcasted_iota(jnp.int32, (SGU_CHUNK, SGU_CHUNK), 0)
              >= lax.broadcasted_iota(jnp.int32, (SGU_CHUNK, SGU_CHUNK), 1))
    blocks = [[] for _ in range(n_chunks)]
    for g in range(SGU_GROUPS):
        ws = jnp.where(causal, w_s_ref[layer, g], 0.0).astype(_BF16)
        rhs = jnp.concatenate([vn[c * SGU_CHUNK:(c + 1) * SGU_CHUNK, g * hd:(g + 1) * hd]
                               for c in range(n_chunks)], axis=1)
        out = _dot(ws, rhs)
        for c in range(n_chunks):
            blocks[c].append(out[:, c * hd:(c + 1) * hd])
    spatial = jnp.concatenate([jnp.concatenate(b, axis=1) + b_s_ref[layer] for b in blocks], axis=0)
    yield
    gate_b = _sigmoid(proj(4 * d, 5 * d))
    yield
    merged = merged + gate_b * _wdot((u * spatial).astype(_BF16), w_pb_ref[...])
    yield
    o = _wdot(merged.astype(_BF16), w_o_ref[...])
    yield
    o_ref[rows, :] = x + _rms(o, g_post_ref[lyr, :])


def _mixer_kernel(x_ref, g_pre_ref, b_in_ref, pool_scale_ref, ln_g_ref, ln_b_ref, w_s_ref, b_s_ref,
                  g_post_ref, w_in_hbm, pool_w_hbm, w_pa_hbm, w_pb_hbm, w_o_hbm, o_ref,
                  w_in_ref, pool_w_ref, w_pa_ref, w_pb_ref, w_o_ref, zbuf_ref, stage_ref, sem_ref,
                  *, layer, tiles_per_seq, lag):
    tm, d = x_ref.shape
    step = pl.program_id(0)
    j = step % tiles_per_seq

    @pl.when(step == 0)
    def _():
        pieces = _matrix_pieces(w_in_hbm.at[layer], w_in_ref)
        for k in range(len(POOL_WINDOWS)):
            pieces += _matrix_pieces(pool_w_hbm.at[layer, k], pool_w_ref.at[k])
        for hbm, ref in ((w_pa_hbm, w_pa_ref), (w_pb_hbm, w_pb_ref), (w_o_hbm, w_o_ref)):
            pieces += _matrix_pieces(hbm.at[layer], ref)
        _load_weights(pieces, stage_ref, sem_ref)

    @pl.when(j == 0)
    def _():
        zbuf_ref[0:HALO, :] = jnp.zeros((HALO, d), _F32)

    @pl.when(j != 0)
    def _():
        zbuf_ref[0:HALO, :] = zbuf_ref[tm:tm + HALO, :]

    _run_staggered(
        [_mixer_program(r, j * tm, layer, x_ref, g_pre_ref, b_in_ref, pool_scale_ref, ln_g_ref,
                        ln_b_ref, w_s_ref, b_s_ref, g_post_ref, o_ref, w_in_ref, pool_w_ref,
                        w_pa_ref, w_pb_ref, w_o_ref, zbuf_ref)
         for r in range(0, tm, SUB_TILE)], lag)


def _ffn_program(row0, layer, x_ref, p_ref, g_pre_ref, g_post_ref, g_ple_ref, o_ref,
                 w1_ref, w2_ref, wg_ref, wp_ref, *, ff_chunk):
    rows = slice(row0, row0 + SUB_TILE)
    lyr = slice(layer, layer + 1)
    x = x_ref[rows, :]
    h = _rms(x, g_pre_ref[lyr, :]).astype(_BF16)
    yield
    f = None
    for c in range(w1_ref.shape[1] // ff_chunk):
        a = _wdot(h, w1_ref[:, c * ff_chunk:(c + 1) * ff_chunk])
        a = jnp.square(jnp.maximum(a, 0.0)).astype(_BF16)
        part = _wdot(a, w2_ref[c * ff_chunk // 2:(c + 1) * ff_chunk // 2, :])
        f = part if f is None else f + part
        yield
    x = x + _rms(f, g_post_ref[lyr, :])
    yield
    gate = _sigmoid(_wdot(x.astype(_BF16), wg_ref[...]))
    e = _wdot(p_ref[rows, :].astype(_BF16), wp_ref[...])
    yield
    o_ref[rows, :] = x + _rms(gate * e, g_ple_ref[lyr, :])


def _ffn_kernel(x_ref, p_ref, g_pre_ref, g_post_ref, g_ple_ref, w1_hbm, w2_hbm, wg_hbm, wp_hbm,
                o_ref, w1_ref, w2_ref, wg_ref, wp_ref, stage_ref, sem_ref, *, layer, ff_chunk, lag):
    @pl.when(pl.program_id(0) == 0)
    def _():
        pieces = []
        for hbm, ref in ((w1_hbm, w1_ref), (w2_hbm, w2_ref), (wg_hbm, wg_ref), (wp_hbm, wp_ref)):
            pieces += _matrix_pieces(hbm.at[layer], ref)
        _load_weights(pieces, stage_ref, sem_ref)

    _run_staggered(
        [_ffn_program(r, layer, x_ref, p_ref, g_pre_ref, g_post_ref, g_ple_ref, o_ref,
                      w1_ref, w2_ref, wg_ref, wp_ref, ff_chunk=ff_chunk)
         for r in range(0, x_ref.shape[0], SUB_TILE)], lag)


def _whole(a):
    nd = a.ndim
    return pl.BlockSpec(a.shape, lambda i: (0,) * nd, pipeline_mode=pl.Buffered(1))


def _packed(k, n):
    return pltpu.VMEM((k // 2, n), _U32)


_HBM = pl.BlockSpec(memory_space=pl.ANY)
_STAGING = [pltpu.VMEM((STAGE_SLOTS, STAGE_ROWS, STAGE_COLS), _F32),
            pltpu.SemaphoreType.DMA((STAGE_SLOTS,))]


def _mixer_call(x, layer, seq, small, big):
    n, d = x.shape
    tm = ROW_TILE
    w_in, pool_w, w_pa, w_pb, w_o = big
    row = pl.BlockSpec((tm, d), lambda i: (i, 0))
    n_groups, gdim = pool_w.shape[1], pool_w.shape[2]
    return pl.pallas_call(
        functools.partial(_mixer_kernel, layer=layer, tiles_per_seq=seq // tm, lag=6),
        out_shape=jax.ShapeDtypeStruct((n, d), _F32),
        grid=(n // tm,),
        in_specs=[row] + [_whole(a) for a in small] + [_HBM] * len(big),
        out_specs=row,
        scratch_shapes=[_packed(d, w_in.shape[2]), pltpu.VMEM((n_groups, gdim // 2, gdim), _U32),
                        _packed(d, d), _packed(d, d), _packed(d, d),
                        pltpu.VMEM((HALO + tm, d), _F32)] + _STAGING,
        compiler_params=pltpu.CompilerParams(
            dimension_semantics=("arbitrary",), vmem_limit_bytes=VMEM_LIMIT_BYTES),
        name="mixer",
    )(x, *small, *big)


def _ffn_call(x, p, layer, small, big):
    n, d = x.shape
    tm = ROW_TILE
    w1, w2, wg, wp = big
    row = pl.BlockSpec((tm, d), lambda i: (i, 0))
    prow = pl.BlockSpec((None, tm, p.shape[2]), lambda i: (layer, i, 0))
    return pl.pallas_call(
        functools.partial(_ffn_kernel, layer=layer, ff_chunk=1024, lag=3),
        out_shape=jax.ShapeDtypeStruct((n, d), _F32),
        grid=(n // tm,),
        in_specs=[row, prow] + [_whole(a) for a in small] + [_HBM] * len(big),
        out_specs=row,
        scratch_shapes=[_packed(d, w1.shape[2]), _packed(w2.shape[1], d), _packed(d, d),
                        _packed(wp.shape[1], d)] + _STAGING,
        compiler_params=pltpu.CompilerParams(
            dimension_semantics=("arbitrary",), vmem_limit_bytes=VMEM_LIMIT_BYTES),
        name="ffn_ple",
    )(x, p, *small, *big)


def kernel(x, p, pre_mix_g, w_in, b_in, pool_w, pool_scale, sgu_ln_g, sgu_ln_b, sgu_w_s,
           sgu_b_s, w_pa, w_pb, w_o, post_mix_g, pre_ffn_g, w_ff1, w_ff2, post_ffn_g,
           w_ple_gate, w_ple_proj, post_ple_g):
    b, s, d = x.shape
    depth = w_in.shape[0]
    assert s % ROW_TILE == 0 and ROW_TILE % SUB_TILE == 0 and SUB_TILE % SGU_CHUNK == 0
    assert d % SGU_GROUPS == 0 and (d // len(POOL_WINDOWS)) % 128 == 0
    xf = x.reshape(b * s, d)
    pf = p.reshape(depth, b * s, p.shape[-1])
    b_s_tile = jnp.repeat(jnp.swapaxes(sgu_b_s, 1, 2), d // SGU_GROUPS, axis=2)
    mixer_small = (pre_mix_g, b_in, pool_scale, sgu_ln_g, sgu_ln_b, sgu_w_s, b_s_tile, post_mix_g)
    mixer_big = (w_in, pool_w, w_pa, w_pb, w_o)
    ffn_small = (pre_ffn_g, post_ffn_g, post_ple_g)
    ffn_big = (w_ff1, w_ff2, w_ple_gate, w_ple_proj)
    for i in range(depth):
        xf = _mixer_call(xf, i, s, mixer_small, mixer_big)
        xf = _ffn_call(xf, pf, i, ffn_small, ffn_big)
    return xf.reshape(b, s, d)
```

```python
import functools
from types import SimpleNamespace

import jax
import jax.numpy as jnp
from jax import lax
from jax.experimental import pallas as pl
from jax.experimental.pallas import tpu as pltpu

POOL_WINDOWS = (2, 4, 8, 16)
SUBLANES = 8
HALO = 16
SGU_CHUNK = 128
SGU_GROUPS = 8
EPS = 1e-6
FF_CHUNK = 1024
SUB_TILE = 256
ROW_TILE = 512
STAGE_LAG = 10
STAGE_ROWS, STAGE_COLS = 256, 1024
STAGE_SLOTS = 4
VMEM_LIMIT_BYTES = 60 * 1024 * 1024

_BF16 = jnp.bfloat16
_F32 = jnp.float32
_U32 = jnp.uint32

_SMALL = ("pre_mix_g", "b_in", "pool_scale", "ln_g", "ln_b", "post_mix_g",
          "pre_ffn_g", "post_ffn_g", "post_ple_g")
_BIG = ("w_in", "pool_w", "w_pa", "w_pb", "w_o", "w_ff1", "w_ff2", "w_gate", "w_proj")


def _rms(x, g):
    return x * lax.rsqrt(jnp.mean(x * x, axis=-1, keepdims=True) + EPS) * g


def _dot(a, b):
    return jnp.dot(a, b, preferred_element_type=_F32)


def _wdot(a, w_packed):
    return _dot(a, pltpu.bitcast(w_packed, _BF16))


_LOG2E = 1.4426950408889634
_GELU_C1 = -2.0 * 0.7978845608028654 * _LOG2E
_GELU_C2 = _GELU_C1 * 0.044715


def _sigmoid(x):
    return 1.0 / (1.0 + jnp.exp2(x * -_LOG2E))


def _gelu_tanh(x):
    return x / (1.0 + jnp.exp2(x * (_GELU_C1 + _GELU_C2 * (x * x))))


def _matrix_pieces(src, dst):
    k, n = src.shape
    pieces = []
    for r in range(0, k, STAGE_ROWS):
        nr = min(STAGE_ROWS, k - r)
        for c in range(0, n, STAGE_COLS):
            nc = min(STAGE_COLS, n - c)
            pieces.append((src.at[pl.ds(r, nr), pl.ds(c, nc)],
                           dst.at[pl.ds(r // 2, nr // 2), pl.ds(c, nc)]))
    return pieces


def _load_weights(pieces, stage_ref, sem_ref):
    slots = stage_ref.shape[0]

    def copy(i):
        src = pieces[i][0]
        nr, nc = src.shape
        return pltpu.make_async_copy(
            src, stage_ref.at[i % slots, pl.ds(0, nr), pl.ds(0, nc)], sem_ref.at[i % slots])

    for i in range(min(slots, len(pieces))):
        copy(i).start()
    for i, (src, dst) in enumerate(pieces):
        nr, nc = src.shape
        copy(i).wait()
        dst[...] = pltpu.bitcast(stage_ref[i % slots, 0:nr, 0:nc].astype(_BF16), _U32)
        if i + slots < len(pieces):
            copy(i + slots).start()


def _run_staggered(programs, lag):
    live = list(enumerate(programs))
    step = 0
    while live:
        for k, prog in list(live):
            if step >= k * lag and next(prog, StopIteration) is StopIteration:
                live.remove((k, prog))
        step += 1


def _window_sums(zext, w):
    s = zext
    k = 1
    while k < min(w, SUBLANES):
        s = s + pltpu.roll(s, k, axis=0)
        k *= 2
    win = s[HALO:, :]
    if w > SUBLANES:
        assert w == 2 * SUBLANES
        win = win + s[HALO - SUBLANES:-SUBLANES, :]
    return win


def _layer_program(row0, seq_row0, layer, x_ref, p_ref, sm, w_s_ref, b_s_ref, wt, o_ref, zbuf_ref):
    tm = SUB_TILE
    d = x_ref.shape[1]
    rows = slice(row0, row0 + tm)
    zrows = slice(HALO + row0, HALO + row0 + tm)
    lyr = slice(layer, layer + 1)

    x = x_ref[rows, :]
    h = _rms(x, sm.pre_mix_g[lyr, :]).astype(_BF16)
    yield

    def proj(lo, hi):
        return _wdot(h, wt.w_in[:, lo:hi]) + sm.b_in[lyr, lo:hi]

    zbuf_ref[zrows, :] = proj(0, d)
    yield
    v = _gelu_tanh(proj(2 * d, 3 * d))
    yield

    t = seq_row0 + row0 + lax.broadcasted_iota(jnp.int32, (tm, 128), 0)
    gdim = d // len(POOL_WINDOWS)
    pooled = []
    for k, w in enumerate(POOL_WINDOWS):
        zext = zbuf_ref[row0:row0 + HALO + tm, k * gdim:(k + 1) * gdim]
        win = _window_sums(zext, w)
        inv_cnt = 1.0 / jnp.minimum(t + 1, w).astype(_F32)
        inv_cnt = jnp.concatenate([inv_cnt] * (gdim // 128), axis=1)
        pooled.append((win * inv_cnt - zext[HALO:, :]).astype(_BF16))
    yield
    gate_a = _sigmoid(proj(3 * d, 4 * d))
    yield
    pm = jnp.concatenate([_wdot(pooled[k], wt.pool_w[k]) for k in range(len(POOL_WINDOWS))], axis=-1)
    pm = (pm * sm.pool_scale[lyr, :]).astype(_BF16)
    merged = gate_a * _wdot(pm, wt.w_pa[...])
    yield

    mu = jnp.mean(v, axis=-1, keepdims=True)
    vc = v - mu
    vn = (vc * lax.rsqrt(jnp.mean(vc * vc, axis=-1, keepdims=True) + EPS) * sm.ln_g[lyr, :]
          + sm.ln_b[lyr, :]).astype(_BF16)
    yield
    u = _gelu_tanh(proj(d, 2 * d))
    yield
    hd = d // SGU_GROUPS
    n_chunks = tm // SGU_CHUNK
    causal = (lax.broadcasted_iota(jnp.int32, (SGU_CHUNK, SGU_CHUNK), 0)
              >= lax.broadcasted_iota(jnp.int32, (SGU_CHUNK, SGU_CHUNK), 1))
    blocks = [[] for _ in range(n_chunks)]
    for g in range(SGU_GROUPS):
        ws = jnp.where(causal, w_s_ref[g], 0.0).astype(_BF16)
        rhs = jnp.concatenate([vn[c * SGU_CHUNK:(c + 1) * SGU_CHUNK, g * hd:(g + 1) * hd]
                               for c in range(n_chunks)], axis=1)
        out = _dot(ws, rhs)
        for c in range(n_chunks):
            blocks[c].append(out[:, c * hd:(c + 1) * hd])
    spatial = jnp.concatenate([jnp.concatenate(b, axis=1) + b_s_ref[...] for b in blocks], axis=0)
    yield
    gate_b = _sigmoid(proj(4 * d, 5 * d))
    yield
    merged = merged + gate_b * _wdot((u * spatial).astype(_BF16), wt.w_pb[...])
    yield
    o = _wdot(merged.astype(_BF16), wt.w_o[...])
    yield
    x = x + _rms(o, sm.post_mix_g[lyr, :])

    h = _rms(x, sm.pre_ffn_g[lyr, :]).astype(_BF16)
    yield
    f = None
    for c in range(0, wt.w_ff1.shape[1], FF_CHUNK):
        a = _wdot(h, wt.w_ff1[:, c:c + FF_CHUNK])
        a = jnp.square(jnp.maximum(a, 0.0)).astype(_BF16)
        part = _wdot(a, wt.w_ff2[c // 2:(c + FF_CHUNK) // 2, :])
        f = part if f is None else f + part
        yield
    x = x + _rms(f, sm.post_ffn_g[lyr, :])
    yield

    gate = _sigmoid(_wdot(x.astype(_BF16), wt.w_gate[...]))
    e = _wdot(p_ref[rows, :].astype(_BF16), wt.w_proj[...])
    yield
    o_ref[rows, :] = x + _rms(gate * e, sm.post_ple_g[lyr, :])


def _layer_kernel(*refs, layer, tiles_per_seq):
    refs = list(refs)
    x_ref, p_ref = refs[:2]
    sm = SimpleNamespace(**dict(zip(_SMALL, refs[2:])))
    k = 2 + len(_SMALL)
    w_s_ref, b_s_ref = refs[k:k + 2]
    hbm = refs[k + 2:k + 2 + len(_BIG)]
    o_ref = refs[k + 2 + len(_BIG)]
    packed = refs[k + 3 + len(_BIG):k + 3 + 2 * len(_BIG)]
    wt = SimpleNamespace(**dict(zip(_BIG, packed)))
    zbuf_ref, stage_ref, sem_ref = refs[k + 3 + 2 * len(_BIG):]

    tm, d = x_ref.shape
    step = pl.program_id(0)
    j = step % tiles_per_seq

    @pl.when(step == 0)
    def _():
        pieces = []
        for name, src, dst in zip(_BIG, hbm, packed):
            if name == "pool_w":
                for g in range(len(POOL_WINDOWS)):
                    pieces += _matrix_pieces(src.at[layer, g], dst.at[g])
            else:
                pieces += _matrix_pieces(src.at[layer], dst)
        _load_weights(pieces, stage_ref, sem_ref)

    @pl.when(j == 0)
    def _():
        zbuf_ref[0:HALO, :] = jnp.zeros((HALO, d), _F32)

    @pl.when(j != 0)
    def _():
        zbuf_ref[0:HALO, :] = zbuf_ref[tm:tm + HALO, :]

    _run_staggered(
        [_layer_program(r, j * tm, layer, x_ref, p_ref, sm, w_s_ref, b_s_ref, wt, o_ref, zbuf_ref)
         for r in range(0, tm, SUB_TILE)], STAGE_LAG)


def _whole(a):
    nd = a.ndim
    return pl.BlockSpec(a.shape, lambda i: (0,) * nd, pipeline_mode=pl.Buffered(1))


def _of_layer(a, layer):
    nd = a.ndim
    return pl.BlockSpec((None,) + a.shape[1:], lambda i: (layer,) + (0,) * (nd - 1),
                        pipeline_mode=pl.Buffered(1))


def _layer_call(x, p, layer, seq, small, w_s, b_s_tile, big):
    n, d = x.shape
    tm = ROW_TILE
    row = pl.BlockSpec((tm, d), lambda i: (i, 0))
    prow = pl.BlockSpec((None, tm, p.shape[2]), lambda i: (layer, i, 0))
    packed = [pltpu.VMEM(w.shape[1:-2] + (w.shape[-2] // 2, w.shape[-1]), _U32) for w in big]
    return pl.pallas_call(
        functools.partial(_layer_kernel, layer=layer, tiles_per_seq=seq // tm),
        out_shape=jax.ShapeDtypeStruct((n, d), _F32),
        grid=(n // tm,),
        in_specs=([row, prow] + [_whole(a) for a in small]
                  + [_of_layer(w_s, layer), _of_layer(b_s_tile, layer)]
                  + [pl.BlockSpec(memory_space=pl.ANY)] * len(big)),
        out_specs=row,
        scratch_shapes=packed + [
            pltpu.VMEM((HALO + tm, d), _F32),
            pltpu.VMEM((STAGE_SLOTS, STAGE_ROWS, STAGE_COLS), _F32),
            pltpu.SemaphoreType.DMA((STAGE_SLOTS,))],
        compiler_params=pltpu.CompilerParams(
            dimension_semantics=("arbitrary",), vmem_limit_bytes=VMEM_LIMIT_BYTES),
        name="layer",
    )(x, p, *small, w_s, b_s_tile, *big)


def kernel(x, p, pre_mix_g, w_in, b_in, pool_w, pool_scale, sgu_ln_g, sgu_ln_b, sgu_w_s,
           sgu_b_s, w_pa, w_pb, w_o, post_mix_g, pre_ffn_g, w_ff1, w_ff2, post_ffn_g,
           w_ple_gate, w_ple_proj, post_ple_g):
    b, s, d = x.shape
    depth = w_in.shape[0]
    assert s % ROW_TILE == 0 and ROW_TILE % SUB_TILE == 0 and SUB_TILE % SGU_CHUNK == 0
    assert d % SGU_GROUPS == 0 and (d // len(POOL_WINDOWS)) % 128 == 0
    assert w_ff1.shape[2] % FF_CHUNK == 0
    xf = x.reshape(b * s, d)
    pf = p.reshape(depth, b * s, p.shape[-1])
    b_s_tile = jnp.repeat(jnp.swapaxes(sgu_b_s, 1, 2), d // SGU_GROUPS, axis=2)
    small = (pre_mix_g, b_in, pool_scale, sgu_ln_g, sgu_ln_b, post_mix_g,
             pre_ffn_g, post_ffn_g, post_ple_g)
    big = (w_in, pool_w, w_pa, w_pb, w_o, w_ff1, w_ff2, w_ple_gate, w_ple_proj)
    for i in range(depth):
        xf = _layer_call(xf, pf, i, s, small, sgu_w_s, b_s_tile, big)
    return xf.reshape(b, s, d)
```

```python
import functools
from types import SimpleNamespace

import jax
import jax.numpy as jnp
from jax import lax
from jax.experimental import pallas as pl
from jax.experimental.pallas import tpu as pltpu

POOL_WINDOWS = (2, 4, 8, 16)
SUBLANES = 8
HALO = 16
SGU_CHUNK = 128
SGU_GROUPS = 8
EPS = 1e-6
FF_CHUNK = 1024
FRONT_FF_CHUNKS = 2
SUB_TILE = 256
ROW_TILE = 2 * SUB_TILE
STAGE_ROWS, STAGE_COLS = 256, 1024
STAGE_SLOTS = 2
VMEM_LIMIT_BYTES = 60 * 1024 * 1024

_BF16 = jnp.bfloat16
_F32 = jnp.float32
_U32 = jnp.uint32

_SMALL = ("pre_mix_g", "b_in", "pool_scale", "ln_g", "ln_b", "post_mix_g",
          "pre_ffn_g", "post_ffn_g", "post_ple_g")
_BIG = ("w_in", "pool_w", "w_pa", "w_pb", "w_o", "w_ff1", "w_ff2", "w_gate", "w_proj")


def _rms(x, g):
    return x * lax.rsqrt(jnp.mean(x * x, axis=-1, keepdims=True) + EPS) * g


def _dot(a, b):
    return jnp.dot(a, b, preferred_element_type=_F32)


def _wdot(a, w_packed):
    return _dot(a, pltpu.bitcast(w_packed, _BF16))


_LOG2E = 1.4426950408889634
_GELU_C1 = -2.0 * 0.7978845608028654 * _LOG2E
_GELU_C2 = _GELU_C1 * 0.044715


def _sigmoid(x):
    return 1.0 / (1.0 + jnp.exp2(x * -_LOG2E))


def _gelu_tanh(x):
    return x / (1.0 + jnp.exp2(x * (_GELU_C1 + _GELU_C2 * (x * x))))


def _matrix_pieces(src, dst):
    k, n = src.shape
    pieces = []
    for r in range(0, k, STAGE_ROWS):
        nr = min(STAGE_ROWS, k - r)
        for c in range(0, n, STAGE_COLS):
            nc = min(STAGE_COLS, n - c)
            pieces.append((src.at[pl.ds(r, nr), pl.ds(c, nc)],
                           dst.at[pl.ds(r // 2, nr // 2), pl.ds(c, nc)]))
    return pieces


def _load_weights(pieces, stage_ref, sem_ref):
    slots = stage_ref.shape[0]

    def copy(i):
        src = pieces[i][0]
        nr, nc = src.shape
        return pltpu.make_async_copy(
            src, stage_ref.at[i % slots, pl.ds(0, nr), pl.ds(0, nc)], sem_ref.at[i % slots])

    for i in range(min(slots, len(pieces))):
        copy(i).start()
    for i, (src, dst) in enumerate(pieces):
        nr, nc = src.shape
        copy(i).wait()
        dst[...] = pltpu.bitcast(stage_ref[i % slots, 0:nr, 0:nc].astype(_BF16), _U32)
        if i + slots < len(pieces):
            copy(i + slots).start()


def _emit_interleaved(schedule):
    live = list(schedule)
    tick = 0
    while live:
        for item in list(live):
            prog, first, period = item
            if tick >= first and (tick - first) % period == 0:
                if next(prog, StopIteration) is StopIteration:
                    live.remove(item)
        tick += 1


def _window_sums(zext, w):
    s = zext
    k = 1
    while k < min(w, SUBLANES):
        s = s + pltpu.roll(s, k, axis=0)
        k *= 2
    win = s[HALO:, :]
    if w > SUBLANES:
        assert w == 2 * SUBLANES
        win = win + s[HALO - SUBLANES:-SUBLANES, :]
    return win


def _ffn_chunk(h, f, c, wt):
    a = _wdot(h, wt.w_ff1[:, c:c + FF_CHUNK])
    a = jnp.square(jnp.maximum(a, 0.0)).astype(_BF16)
    part = _wdot(a, wt.w_ff2[c // 2:(c + FF_CHUNK) // 2, :])
    return part if f is None else f + part


def _front_stages(row0, seq_row0, layer, x_ref, sm, w_s_ref, b_s_ref, wt, zbuf_ref):
    tm = SUB_TILE
    d = x_ref.shape[1]
    rows = slice(row0, row0 + tm)
    zrows = slice(HALO + row0, HALO + row0 + tm)
    lyr = slice(layer, layer + 1)

    x = x_ref[rows, :]
    h = _rms(x, sm.pre_mix_g[lyr, :]).astype(_BF16)
    yield

    def proj(lo, hi):
        return _wdot(h, wt.w_in[:, lo:hi]) + sm.b_in[lyr, lo:hi]

    zbuf_ref[zrows, :] = proj(0, d)
    yield
    v = _gelu_tanh(proj(2 * d, 3 * d))
    yield

    t = seq_row0 + row0 + lax.broadcasted_iota(jnp.int32, (tm, 128), 0)
    gdim = d // len(POOL_WINDOWS)
    pooled = []
    for k, w in enumerate(POOL_WINDOWS):
        zext = zbuf_ref[row0:row0 + HALO + tm, k * gdim:(k + 1) * gdim]
        win = _window_sums(zext, w)
        inv_cnt = 1.0 / jnp.minimum(t + 1, w).astype(_F32)
        inv_cnt = jnp.concatenate([inv_cnt] * (gdim // 128), axis=1)
        pooled.append((win * inv_cnt - zext[HALO:, :]).astype(_BF16))
    yield
    gate_a = _sigmoid(proj(3 * d, 4 * d))
    yield
    pm = jnp.concatenate([_wdot(pooled[k], wt.pool_w[k]) for k in range(len(POOL_WINDOWS))], axis=-1)
    pm = (pm * sm.pool_scale[lyr, :]).astype(_BF16)
    merged = gate_a * _wdot(pm, wt.w_pa[...])
    yield

    mu = jnp.mean(v, axis=-1, keepdims=True)
    vc = v - mu
    vn = (vc * lax.rsqrt(jnp.mean(vc * vc, axis=-1, keepdims=True) + EPS) * sm.ln_g[lyr, :]
          + sm.ln_b[lyr, :]).astype(_BF16)
    yield
    u = _gelu_tanh(proj(d, 2 * d))
    yield
    hd = d // SGU_GROUPS
    n_chunks = tm // SGU_CHUNK
    causal = (lax.broadcasted_iota(jnp.int32, (SGU_CHUNK, SGU_CHUNK), 0)
              >= lax.broadcasted_iota(jnp.int32, (SGU_CHUNK, SGU_CHUNK), 1))
    blocks = [[] for _ in range(n_chunks)]
    for g in range(SGU_GROUPS):
        ws = jnp.where(causal, w_s_ref[g], 0.0).astype(_BF16)
        rhs = jnp.concatenate([vn[c * SGU_CHUNK:(c + 1) * SGU_CHUNK, g * hd:(g + 1) * hd]
                               for c in range(n_chunks)], axis=1)
        out = _dot(ws, rhs)
        for c in range(n_chunks):
            blocks[c].append(out[:, c * hd:(c + 1) * hd])
    spatial = jnp.concatenate([jnp.concatenate(b, axis=1) + b_s_ref[...] for b in blocks], axis=0)
    yield
    gate_b = _sigmoid(proj(4 * d, 5 * d))
    yield
    merged = merged + gate_b * _wdot((u * spatial).astype(_BF16), wt.w_pb[...])
    yield
    o = _wdot(merged.astype(_BF16), wt.w_o[...])
    yield
    x = x + _rms(o, sm.post_mix_g[lyr, :])
    h = _rms(x, sm.pre_ffn_g[lyr, :]).astype(_BF16)
    yield
    f = None
    for c in range(0, FRONT_FF_CHUNKS * FF_CHUNK, FF_CHUNK):
        f = _ffn_chunk(h, f, c, wt)
        yield
    return x, h, f


def _back_stages(x, h, f, p_bf16, layer, sm, wt):
    lyr = slice(layer, layer + 1)
    for c in range(FRONT_FF_CHUNKS * FF_CHUNK, wt.w_ff1.shape[1], FF_CHUNK):
        f = _ffn_chunk(h, f, c, wt)
        yield
    x = x + _rms(f, sm.post_ffn_g[lyr, :])
    yield
    gate = _sigmoid(_wdot(x.astype(_BF16), wt.w_gate[...]))
    e = _wdot(p_bf16, wt.w_proj[...])
    yield
    return x + _rms(gate * e, sm.post_ple_g[lyr, :])


def _layer_kernel(*refs, layer, tiles_per_seq, n_tiles):
    refs = list(refs)
    x_ref, p_ref, p_prev_ref = refs[:3]
    sm = SimpleNamespace(**dict(zip(_SMALL, refs[3:])))
    k = 3 + len(_SMALL)
    w_s_ref, b_s_ref = refs[k:k + 2]
    hbm = refs[k + 2:k + 2 + len(_BIG)]
    out_hbm = refs[k + 2 + len(_BIG)]
    packed = refs[k + 3 + len(_BIG):k + 3 + 2 * len(_BIG)]
    wt = SimpleNamespace(**dict(zip(_BIG, packed)))
    (zbuf_ref, carry_x, carry_h, carry_f, obuf_ref, osem_ref,
     stage_ref, sem_ref) = refs[k + 3 + 2 * len(_BIG):]

    tm, d = x_ref.shape
    step = pl.program_id(0)
    j = step % tiles_per_seq

    def out_copy(which, sub_tile):
        row = pl.multiple_of(sub_tile * SUB_TILE, SUB_TILE)
        return pltpu.make_async_copy(obuf_ref.at[which], out_hbm.at[pl.ds(row, SUB_TILE), :],
                                     osem_ref.at[which])

    @pl.when(step == 0)
    def _():
        pieces = []
        for name, src, dst in zip(_BIG, hbm, packed):
            if name == "pool_w":
                for g in range(len(POOL_WINDOWS)):
                    pieces += _matrix_pieces(src.at[layer, g], dst.at[g])
            else:
                pieces += _matrix_pieces(src.at[layer], dst)
        _load_weights(pieces, stage_ref, sem_ref)
        carry_x[...] = jnp.zeros(carry_x.shape, carry_x.dtype)
        carry_h[...] = jnp.zeros(carry_h.shape, carry_h.dtype)
        carry_f[...] = jnp.zeros(carry_f.shape, carry_f.dtype)

    @pl.when(step >= 1)
    def _():
        out_copy(1, 2 * (step - 1)).wait()

    @pl.when(step >= 2)
    def _():
        out_copy(0, 2 * (step - 1) - 1).wait()

    def carried_program():
        state = (carry_x[...], carry_h[...], carry_f[...], p_prev_ref[...].astype(_BF16))
        yield
        obuf_ref[0] = yield from _back_stages(*state, layer, sm, wt)

    def lead_program():
        x, h, f = yield from _front_stages(0, j * tm, layer, x_ref, sm, w_s_ref, b_s_ref, wt, zbuf_ref)
        p_bf16 = p_ref[0:SUB_TILE, :].astype(_BF16)
        obuf_ref[1] = yield from _back_stages(x, h, f, p_bf16, layer, sm, wt)

    def trail_program():
        x, h, f = yield from _front_stages(SUB_TILE, j * tm, layer, x_ref, sm, w_s_ref, b_s_ref,
                                           wt, zbuf_ref)
        carry_x[...] = x
        carry_h[...] = h
        carry_f[...] = f

    @pl.when(step < n_tiles)
    def _():
        @pl.when(j == 0)
        def _():
            zbuf_ref[0:HALO, :] = jnp.zeros((HALO, d), _F32)

        @pl.when(j != 0)
        def _():
            zbuf_ref[0:HALO, :] = zbuf_ref[tm:tm + HALO, :]

        _emit_interleaved([(carried_program(), 0, 2), (lead_program(), 0, 1), (trail_program(), 4, 1)])
        out_copy(1, 2 * step).start()

        @pl.when(step >= 1)
        def _():
            out_copy(0, 2 * step - 1).start()

    @pl.when(step == n_tiles)
    def _():
        _emit_interleaved([(carried_program(), 0, 1)])
        out_copy(0, 2 * step - 1).start()
        out_copy(0, 2 * step - 1).wait()


def _whole(a):
    nd = a.ndim
    return pl.BlockSpec(a.shape, lambda i: (0,) * nd, pipeline_mode=pl.Buffered(1))


def _of_layer(a, layer):
    nd = a.ndim
    return pl.BlockSpec((None,) + a.shape[1:], lambda i: (layer,) + (0,) * (nd - 1),
                        pipeline_mode=pl.Buffered(1))


def _layer_call(x, p, layer, seq, small, w_s, b_s_tile, big):
    n, d = x.shape
    tm = ROW_TILE
    n_tiles = n // tm
    assert n_tiles >= 2
    last = n_tiles - 1
    row = pl.BlockSpec((tm, d), lambda i: (jnp.minimum(i, last), 0))
    prow = pl.BlockSpec((None, tm, p.shape[2]), lambda i: (layer, jnp.minimum(i, last), 0))
    prow_prev = pl.BlockSpec((None, SUB_TILE, p.shape[2]),
                             lambda i: (layer, jnp.maximum(2 * i - 1, 0), 0))
    packed = [pltpu.VMEM(w.shape[1:-2] + (w.shape[-2] // 2, w.shape[-1]), _U32) for w in big]
    return pl.pallas_call(
        functools.partial(_layer_kernel, layer=layer, tiles_per_seq=seq // tm, n_tiles=n_tiles),
        out_shape=jax.ShapeDtypeStruct((n, d), _F32),
        grid=(n_tiles + 1,),
        in_specs=([row, prow, prow_prev] + [_whole(a) for a in small]
                  + [_of_layer(w_s, layer), _of_layer(b_s_tile, layer)]
                  + [pl.BlockSpec(memory_space=pl.ANY)] * len(big)),
        out_specs=pl.BlockSpec(memory_space=pl.ANY),
        scratch_shapes=packed + [
            pltpu.VMEM((HALO + tm, d), _F32),
            pltpu.VMEM((SUB_TILE, d), _F32),
            pltpu.VMEM((SUB_TILE, d), _BF16),
            pltpu.VMEM((SUB_TILE, d), _F32),
            pltpu.VMEM((2, SUB_TILE, d), _F32),
            pltpu.SemaphoreType.DMA((2,)),
            pltpu.VMEM((STAGE_SLOTS, STAGE_ROWS, STAGE_COLS), _F32),
            pltpu.SemaphoreType.DMA((STAGE_SLOTS,))],
        compiler_params=pltpu.CompilerParams(
            dimension_semantics=("arbitrary",), vmem_limit_bytes=VMEM_LIMIT_BYTES),
        name="layer",
    )(x, p, p, *small, w_s, b_s_tile, *big)


def kernel(x, p, pre_mix_g, w_in, b_in, pool_w, pool_scale, sgu_ln_g, sgu_ln_b, sgu_w_s,
           sgu_b_s, w_pa, w_pb, w_o, post_mix_g, pre_ffn_g, w_ff1, w_ff2, post_ffn_g,
           w_ple_gate, w_ple_proj, post_ple_g):
    b, s, d = x.shape
    depth = w_in.shape[0]
    assert s % ROW_TILE == 0 and SUB_TILE % SGU_CHUNK == 0
    assert d % SGU_GROUPS == 0 and (d // len(POOL_WINDOWS)) % 128 == 0
    assert w_ff1.shape[2] % FF_CHUNK == 0
    xf = x.reshape(b * s, d)
    pf = p.reshape(depth, b * s, p.shape[-1])
    b_s_tile = jnp.repeat(jnp.swapaxes(sgu_b_s, 1, 2), d // SGU_GROUPS, axis=2)
    small = (pre_mix_g, b_in, pool_scale, sgu_ln_g, sgu_ln_b, post_mix_g,
             pre_ffn_g, post_ffn_g, post_ple_g)
    big = (w_in, pool_w, w_pa, w_pb, w_o, w_ff1, w_ff2, w_ple_gate, w_ple_proj)
    for i in range(depth):
        xf = _layer_call(xf, pf, i, s, small, sgu_w_s, b_s_tile, big)
    return xf.reshape(b, s, d)
```

```python
import functools
from types import SimpleNamespace

import jax
import jax.numpy as jnp
from jax import lax
from jax.experimental import pallas as pl
from jax.experimental.pallas import tpu as pltpu

POOL_WINDOWS = (2, 4, 8, 16)
SUBLANES = 8
HALO = 16
SGU_CHUNK = 128
SGU_GROUPS = 8
EPS = 1e-6
FF_CHUNK = 1024
FRONT_FF_CHUNKS = 2
SUB_TILE = 256
ROW_TILE = 2 * SUB_TILE
STAGE_ROWS, STAGE_COLS = SUB_TILE, 1024
STAGE_SLOTS = 4
VMEM_LIMIT_BYTES = 60 * 1024 * 1024

_BF16 = jnp.bfloat16
_F32 = jnp.float32
_U32 = jnp.uint32

_SMALL = ("pre_mix_g", "b_in", "pool_scale", "ln_g", "ln_b", "post_mix_g",
          "pre_ffn_g", "post_ffn_g", "post_ple_g")
_BIG = ("w_in", "pool_w", "w_pa", "w_pb", "w_o", "w_ff1", "w_ff2", "w_gate", "w_proj")


def _rms(x, g):
    return x * lax.rsqrt(jnp.mean(x * x, axis=-1, keepdims=True) + EPS) * g


def _dot(a, b):
    return jnp.dot(a, b, preferred_element_type=_F32)


def _wdot(a, w_packed):
    return _dot(a, pltpu.bitcast(w_packed, _BF16))


_LOG2E = 1.4426950408889634
_GELU_C1 = -2.0 * 0.7978845608028654 * _LOG2E
_GELU_C2 = _GELU_C1 * 0.044715


def _sigmoid(x):
    return 1.0 / (1.0 + jnp.exp2(x * -_LOG2E))


def _gelu_tanh(x):
    return x / (1.0 + jnp.exp2(x * (_GELU_C1 + _GELU_C2 * (x * x))))


def _matrix_pieces(src, dst):
    k, n = src.shape
    pieces = []
    for r in range(0, k, STAGE_ROWS):
        nr = min(STAGE_ROWS, k - r)
        for c in range(0, n, STAGE_COLS):
            nc = min(STAGE_COLS, n - c)
            pieces.append((src.at[pl.ds(r, nr), pl.ds(c, nc)],
                           dst.at[pl.ds(r // 2, nr // 2), pl.ds(c, nc)]))
    return pieces


def _load_weights(pieces, stage_ref, sem_ref):
    slots = stage_ref.shape[0]

    def copy(i):
        src = pieces[i][0]
        nr, nc = src.shape
        return pltpu.make_async_copy(
            src, stage_ref.at[i % slots, pl.ds(0, nr), pl.ds(0, nc)], sem_ref.at[i % slots])

    for i in range(min(slots, len(pieces))):
        copy(i).start()
    for i, (src, dst) in enumerate(pieces):
        nr, nc = src.shape
        copy(i).wait()
        dst[...] = pltpu.bitcast(stage_ref[i % slots, 0:nr, 0:nc].astype(_BF16), _U32)
        if i + slots < len(pieces):
            copy(i + slots).start()


def _emit_interleaved(schedule):
    live = list(schedule)
    tick = 0
    while live:
        for item in list(live):
            prog, first, period = item
            if tick >= first and (tick - first) % period == 0:
                if next(prog, StopIteration) is StopIteration:
                    live.remove(item)
        tick += 1


def _window_sums(zext, w):
    s = zext
    k = 1
    while k < min(w, SUBLANES):
        s = s + pltpu.roll(s, k, axis=0)
        k *= 2
    win = s[HALO:, :]
    if w > SUBLANES:
        assert w == 2 * SUBLANES
        win = win + s[HALO - SUBLANES:-SUBLANES, :]
    return win


def _ffn_chunk(h, f, c, wt):
    a = _wdot(h, wt.w_ff1[:, c:c + FF_CHUNK])
    a = jnp.square(jnp.maximum(a, 0.0)).astype(_BF16)
    part = _wdot(a, wt.w_ff2[c // 2:(c + FF_CHUNK) // 2, :])
    return part if f is None else f + part


def _front_stages(row0, seq_row0, layer, x_ref, sm, w_s_ref, b_s_ref, wt, zbuf_ref):
    tm = SUB_TILE
    d = x_ref.shape[1]
    rows = slice(row0, row0 + tm)
    zrows = slice(HALO + row0, HALO + row0 + tm)
    lyr = slice(layer, layer + 1)

    x = x_ref[rows, :]
    h = _rms(x, sm.pre_mix_g[lyr, :]).astype(_BF16)
    yield

    def proj(lo, hi):
        return _wdot(h, wt.w_in[:, lo:hi]) + sm.b_in[lyr, lo:hi]

    zbuf_ref[zrows, :] = proj(0, d)
    yield
    v = _gelu_tanh(proj(2 * d, 3 * d))
    yield

    t = seq_row0 + row0 + lax.broadcasted_iota(jnp.int32, (tm, 128), 0)
    gdim = d // len(POOL_WINDOWS)
    pooled = []
    for k, w in enumerate(POOL_WINDOWS):
        zext = zbuf_ref[row0:row0 + HALO + tm, k * gdim:(k + 1) * gdim]
        win = _window_sums(zext, w)
        inv_cnt = 1.0 / jnp.minimum(t + 1, w).astype(_F32)
        inv_cnt = jnp.concatenate([inv_cnt] * (gdim // 128), axis=1)
        pooled.append((win * inv_cnt - zext[HALO:, :]).astype(_BF16))
    yield
    gate_a = _sigmoid(proj(3 * d, 4 * d))
    yield
    pm = jnp.concatenate([_wdot(pooled[k], wt.pool_w[k]) for k in range(len(POOL_WINDOWS))], axis=-1)
    pm = (pm * sm.pool_scale[lyr, :]).astype(_BF16)
    merged = gate_a * _wdot(pm, wt.w_pa[...])
    yield

    mu = jnp.mean(v, axis=-1, keepdims=True)
    vc = v - mu
    vn = (vc * lax.rsqrt(jnp.mean(vc * vc, axis=-1, keepdims=True) + EPS) * sm.ln_g[lyr, :]
          + sm.ln_b[lyr, :]).astype(_BF16)
    yield
    u = _gelu_tanh(proj(d, 2 * d))
    yield
    hd = d // SGU_GROUPS
    n_chunks = tm // SGU_CHUNK
    causal = (lax.broadcasted_iota(jnp.int32, (SGU_CHUNK, SGU_CHUNK), 0)
              >= lax.broadcasted_iota(jnp.int32, (SGU_CHUNK, SGU_CHUNK), 1))
    blocks = [[] for _ in range(n_chunks)]
    for g in range(SGU_GROUPS):
        ws = jnp.where(causal, w_s_ref[g], 0.0).astype(_BF16)
        rhs = jnp.concatenate([vn[c * SGU_CHUNK:(c + 1) * SGU_CHUNK, g * hd:(g + 1) * hd]
                               for c in range(n_chunks)], axis=1)
        out = _dot(ws, rhs)
        for c in range(n_chunks):
            blocks[c].append(out[:, c * hd:(c + 1) * hd])
    spatial = jnp.concatenate([jnp.concatenate(b, axis=1) + b_s_ref[...] for b in blocks], axis=0)
    yield
    gate_b = _sigmoid(proj(4 * d, 5 * d))
    yield
    merged = merged + gate_b * _wdot((u * spatial).astype(_BF16), wt.w_pb[...])
    yield
    o = _wdot(merged.astype(_BF16), wt.w_o[...])
    yield
    x = x + _rms(o, sm.post_mix_g[lyr, :])
    h = _rms(x, sm.pre_ffn_g[lyr, :]).astype(_BF16)
    yield
    f = None
    for c in range(0, FRONT_FF_CHUNKS * FF_CHUNK, FF_CHUNK):
        f = _ffn_chunk(h, f, c, wt)
        yield
    return x, h, f


def _back_stages(x, h, f, p_bf16, layer, sm, wt):
    lyr = slice(layer, layer + 1)
    for c in range(FRONT_FF_CHUNKS * FF_CHUNK, wt.w_ff1.shape[1], FF_CHUNK):
        f = _ffn_chunk(h, f, c, wt)
        yield
    x = x + _rms(f, sm.post_ffn_g[lyr, :])
    yield
    gate = _sigmoid(_wdot(x.astype(_BF16), wt.w_gate[...]))
    e = _wdot(p_bf16, wt.w_proj[...])
    yield
    return x + _rms(gate * e, sm.post_ple_g[lyr, :])


def _layer_kernel(*refs, layer, tiles_per_seq, n_tiles):
    refs = list(refs)
    x_ref, p_ref, p_prev_ref = refs[:3]
    sm = SimpleNamespace(**dict(zip(_SMALL, refs[3:])))
    k = 3 + len(_SMALL)
    w_s_ref, b_s_ref = refs[k:k + 2]
    hbm = refs[k + 2:k + 2 + len(_BIG)]
    out_hbm = refs[k + 2 + len(_BIG)]
    packed = refs[k + 3 + len(_BIG):k + 3 + 2 * len(_BIG)]
    wt = SimpleNamespace(**dict(zip(_BIG, packed)))
    zbuf_ref, carry_x, carry_h, carry_f, stage_ref, osem_ref, wsem_ref = refs[k + 3 + 2 * len(_BIG):]

    tm, d = x_ref.shape
    assert stage_ref.shape == (4, SUB_TILE, d)
    step = pl.program_id(0)
    j = step % tiles_per_seq
    par = step % 2

    def out_slot(parity, which):
        return 2 * parity + which

    def out_copy(parity, which, sub_tile):
        row = pl.multiple_of(sub_tile * SUB_TILE, SUB_TILE)
        slot = out_slot(parity, which)
        return pltpu.make_async_copy(stage_ref.at[slot], out_hbm.at[pl.ds(row, SUB_TILE), :],
                                     osem_ref.at[slot])

    @pl.when(step == 0)
    def _():
        pieces = []
        for name, src, dst in zip(_BIG, hbm, packed):
            if name == "pool_w":
                for g in range(len(POOL_WINDOWS)):
                    pieces += _matrix_pieces(src.at[layer, g], dst.at[g])
            else:
                pieces += _matrix_pieces(src.at[layer], dst)
        _load_weights(pieces, stage_ref, wsem_ref)
        carry_x[...] = jnp.zeros(carry_x.shape, carry_x.dtype)
        carry_h[...] = jnp.zeros(carry_h.shape, carry_h.dtype)
        carry_f[...] = jnp.zeros(carry_f.shape, carry_f.dtype)

    @pl.when(step >= 2)
    def _():
        out_copy(par, 1, 2 * (step - 2)).wait()

    @pl.when(step >= 3)
    def _():
        out_copy(par, 0, 2 * (step - 2) - 1).wait()

    def carried_program():
        state = (carry_x[...], carry_h[...], carry_f[...], p_prev_ref[...].astype(_BF16))
        yield
        stage_ref[out_slot(par, 0)] = yield from _back_stages(*state, layer, sm, wt)

    def lead_program():
        x, h, f = yield from _front_stages(0, j * tm, layer, x_ref, sm, w_s_ref, b_s_ref, wt, zbuf_ref)
        p_bf16 = p_ref[0:SUB_TILE, :].astype(_BF16)
        stage_ref[out_slot(par, 1)] = yield from _back_stages(x, h, f, p_bf16, layer, sm, wt)

    def trail_program():
        x, h, f = yield from _front_stages(SUB_TILE, j * tm, layer, x_ref, sm, w_s_ref, b_s_ref,
                                           wt, zbuf_ref)
        carry_x[...] = x
        carry_h[...] = h
        carry_f[...] = f

    @pl.when(step < n_tiles)
    def _():
        @pl.when(j == 0)
        def _():
            zbuf_ref[0:HALO, :] = jnp.zeros((HALO, d), _F32)

        @pl.when(j != 0)
        def _():
            zbuf_ref[0:HALO, :] = zbuf_ref[tm:tm + HALO, :]

        _emit_interleaved([(carried_program(), 0, 2), (lead_program(), 0, 1), (trail_program(), 4, 1)])
        out_copy(par, 1, 2 * step).start()

        @pl.when(step >= 1)
        def _():
            out_copy(par, 0, 2 * step - 1).start()

    @pl.when(step == n_tiles)
    def _():
        _emit_interleaved([(carried_program(), 0, 1)])
        out_copy(par, 0, 2 * step - 1).start()
        out_copy(par, 0, 2 * step - 1).wait()
        out_copy(1 - par, 1, 2 * (step - 1)).wait()
        out_copy(1 - par, 0, 2 * (step - 1) - 1).wait()


def _whole(a):
    nd = a.ndim
    return pl.BlockSpec(a.shape, lambda i: (0,) * nd, pipeline_mode=pl.Buffered(1))


def _of_layer(a, layer):
    nd = a.ndim
    return pl.BlockSpec((None,) + a.shape[1:], lambda i: (layer,) + (0,) * (nd - 1),
                        pipeline_mode=pl.Buffered(1))


def _layer_call(x, p, layer, seq, small, w_s, b_s_tile, big):
    n, d = x.shape
    tm = ROW_TILE
    n_tiles = n // tm
    assert n_tiles >= 2
    last = n_tiles - 1
    row = pl.BlockSpec((tm, d), lambda i: (jnp.minimum(i, last), 0))
    prow = pl.BlockSpec((None, tm, p.shape[2]), lambda i: (layer, jnp.minimum(i, last), 0))
    prow_prev = pl.BlockSpec((None, SUB_TILE, p.shape[2]),
                             lambda i: (layer, jnp.maximum(2 * i - 1, 0), 0))
    packed = [pltpu.VMEM(w.shape[1:-2] + (w.shape[-2] // 2, w.shape[-1]), _U32) for w in big]
    return pl.pallas_call(
        functools.partial(_layer_kernel, layer=layer, tiles_per_seq=seq // tm, n_tiles=n_tiles),
        out_shape=jax.ShapeDtypeStruct((n, d), _F32),
        grid=(n_tiles + 1,),
        in_specs=([row, prow, prow_prev] + [_whole(a) for a in small]
                  + [_of_layer(w_s, layer), _of_layer(b_s_tile, layer)]
                  + [pl.BlockSpec(memory_space=pl.ANY)] * len(big)),
        out_specs=pl.BlockSpec(memory_space=pl.ANY),
        scratch_shapes=packed + [
            pltpu.VMEM((HALO + tm, d), _F32),
            pltpu.VMEM((SUB_TILE, d), _F32),
            pltpu.VMEM((SUB_TILE, d), _BF16),
            pltpu.VMEM((SUB_TILE, d), _F32),
            pltpu.VMEM((STAGE_SLOTS, STAGE_ROWS, STAGE_COLS), _F32),
            pltpu.SemaphoreType.DMA((STAGE_SLOTS,)),
            pltpu.SemaphoreType.DMA((STAGE_SLOTS,))],
        compiler_params=pltpu.CompilerParams(
            dimension_semantics=("arbitrary",), vmem_limit_bytes=VMEM_LIMIT_BYTES),
        name="layer",
    )(x, p, p, *small, w_s, b_s_tile, *big)


def kernel(x, p, pre_mix_g, w_in, b_in, pool_w, pool_scale, sgu_ln_g, sgu_ln_b, sgu_w_s,
           sgu_b_s, w_pa, w_pb, w_o, post_mix_g, pre_ffn_g, w_ff1, w_ff2, post_ffn_g,
           w_ple_gate, w_ple_proj, post_ple_g):
    b, s, d = x.shape
    depth = w_in.shape[0]
    assert s % ROW_TILE == 0 and SUB_TILE % SGU_CHUNK == 0
    assert d % SGU_GROUPS == 0 and (d // len(POOL_WINDOWS)) % 128 == 0
    assert w_ff1.shape[2] % FF_CHUNK == 0
    xf = x.reshape(b * s, d)
    pf = p.reshape(depth, b * s, p.shape[-1])
    b_s_tile = jnp.repeat(jnp.swapaxes(sgu_b_s, 1, 2), d // SGU_GROUPS, axis=2)
    small = (pre_mix_g, b_in, pool_scale, sgu_ln_g, sgu_ln_b, post_mix_g,
             pre_ffn_g, post_ffn_g, post_ple_g)
    big = (w_in, pool_w, w_pa, w_pb, w_o, w_ff1, w_ff2, w_ple_gate, w_ple_proj)
    for i in range(depth):
        xf = _layer_call(xf, pf, i, s, small, sgu_w_s, b_s_tile, big)
    return xf.reshape(b, s, d)
```

```python
import functools
from types import SimpleNamespace

import jax
import jax.numpy as jnp
from jax import lax
from jax.experimental import pallas as pl
from jax.experimental.pallas import tpu as pltpu

POOL_WINDOWS = (2, 4, 8, 16)
SUBLANES = 8
HALO = 16
SGU_CHUNK = 128
SGU_GROUPS = 8
EPS = 1e-6
FF_CHUNK = 1024
FRONT_FF_CHUNKS = 2
SUB_TILE = 256
ROW_TILE = 2 * SUB_TILE
STAGE_ROWS, STAGE_COLS = SUB_TILE, 1024
STAGE_SLOTS = 4
VMEM_LIMIT_BYTES = 60 * 1024 * 1024

_BF16 = jnp.bfloat16
_F32 = jnp.float32
_U32 = jnp.uint32

_SMALL = ("pre_mix_g", "b_in", "pool_scale", "ln_g", "ln_b", "post_mix_g",
          "pre_ffn_g", "post_ffn_g", "post_ple_g")
_BIG = ("w_in", "pool_w", "w_pa", "w_pb", "w_o", "w_ff1", "w_ff2", "w_gate", "w_proj")


def _rms(x, g):
    return x * lax.rsqrt(jnp.mean(x * x, axis=-1, keepdims=True) + EPS) * g


def _dot(a, b):
    return jnp.dot(a, b, preferred_element_type=_F32)


def _wdot(a, w_packed):
    return _dot(a, pltpu.bitcast(w_packed, _BF16))


_LOG2E = 1.4426950408889634
_GELU_C1 = -2.0 * 0.7978845608028654 * _LOG2E
_GELU_C2 = _GELU_C1 * 0.044715


def _sigmoid(x):
    return 1.0 / (1.0 + jnp.exp2(x * -_LOG2E))


def _gelu_tanh(x):
    return x / (1.0 + jnp.exp2(x * (_GELU_C1 + _GELU_C2 * (x * x))))


def _block_pieces(src, dst, rows, cols):
    pieces = []
    for r in range(rows.start, rows.stop, STAGE_ROWS):
        nr = min(STAGE_ROWS, rows.stop - r)
        for c in range(cols.start, cols.stop, STAGE_COLS):
            nc = min(STAGE_COLS, cols.stop - c)
            pieces.append((src.at[pl.ds(r, nr), pl.ds(c, nc)],
                           dst.at[pl.ds(r // 2, nr // 2), pl.ds(c, nc)]))
    return pieces


class _WeightLoader:
    def __init__(self, stage_ref, sem_ref):
        self.stage_ref, self.sem_ref = stage_ref, sem_ref
        self.slots = stage_ref.shape[0]
        self.pieces, self.group_end = [], {}
        self.started = self.done = 0

    def add(self, tag, pieces):
        self.pieces += pieces
        self.group_end[tag] = len(self.pieces)

    def _copy(self, i):
        src = self.pieces[i][0]
        nr, nc = src.shape
        slot = i % self.slots
        return pltpu.make_async_copy(
            src, self.stage_ref.at[slot, pl.ds(0, nr), pl.ds(0, nc)], self.sem_ref.at[slot])

    def _start_next(self):
        if self.started < len(self.pieces):
            self._copy(self.started).start()
            self.started += 1

    def start(self):
        for _ in range(self.slots):
            self._start_next()

    def _advance_to(self, n):
        while self.done < n:
            i = self.done
            src, dst = self.pieces[i]
            nr, nc = src.shape
            self._copy(i).wait()
            dst[...] = pltpu.bitcast(self.stage_ref[i % self.slots, 0:nr, 0:nc].astype(_BF16), _U32)
            self.done += 1
            self._start_next()

    def ready(self, *tag):
        self._advance_to(self.group_end[tag])

    def pump(self, n):
        self._advance_to(min(self.done + n, len(self.pieces)))

    def finish(self):
        self._advance_to(len(self.pieces))


def _emit_interleaved(schedule, after_tick=None):
    live = list(schedule)
    tick = 0
    while live:
        for item in list(live):
            prog, first, period = item
            if tick >= first and (tick - first) % period == 0:
                if next(prog, StopIteration) is StopIteration:
                    live.remove(item)
        if after_tick is not None:
            after_tick()
        tick += 1


def _window_sums(zext, w):
    s = zext
    k = 1
    while k < min(w, SUBLANES):
        s = s + pltpu.roll(s, k, axis=0)
        k *= 2
    win = s[HALO:, :]
    if w > SUBLANES:
        assert w == 2 * SUBLANES
        win = win + s[HALO - SUBLANES:-SUBLANES, :]
    return win


def _ffn_chunk(h, f, c, wt):
    wt.ready("w_ff", c)
    a = _wdot(h, wt.w_ff1[:, c:c + FF_CHUNK])
    a = jnp.square(jnp.maximum(a, 0.0)).astype(_BF16)
    part = _wdot(a, wt.w_ff2[c // 2:(c + FF_CHUNK) // 2, :])
    return part if f is None else f + part


def _front_stages(row0, seq_row0, layer, x_ref, sm, w_s_ref, b_s_ref, wt, zbuf_ref):
    tm = SUB_TILE
    d = x_ref.shape[1]
    rows = slice(row0, row0 + tm)
    zrows = slice(HALO + row0, HALO + row0 + tm)
    lyr = slice(layer, layer + 1)

    x = x_ref[rows, :]
    h = _rms(x, sm.pre_mix_g[lyr, :]).astype(_BF16)
    yield

    def proj(lo, hi):
        wt.ready("w_in", lo)
        return _wdot(h, wt.w_in[:, lo:hi]) + sm.b_in[lyr, lo:hi]

    zbuf_ref[zrows, :] = proj(0, d)
    yield
    v = _gelu_tanh(proj(2 * d, 3 * d))
    yield

    t = seq_row0 + row0 + lax.broadcasted_iota(jnp.int32, (tm, 128), 0)
    gdim = d // len(POOL_WINDOWS)
    pooled = []
    for k, w in enumerate(POOL_WINDOWS):
        zext = zbuf_ref[row0:row0 + HALO + tm, k * gdim:(k + 1) * gdim]
        win = _window_sums(zext, w)
        inv_cnt = 1.0 / jnp.minimum(t + 1, w).astype(_F32)
        inv_cnt = jnp.concatenate([inv_cnt] * (gdim // 128), axis=1)
        pooled.append((win * inv_cnt - zext[HALO:, :]).astype(_BF16))
    yield
    gate_a = _sigmoid(proj(3 * d, 4 * d))
    yield
    wt.ready("w_pa")
    pm = jnp.concatenate([_wdot(pooled[k], wt.pool_w[k]) for k in range(len(POOL_WINDOWS))], axis=-1)
    pm = (pm * sm.pool_scale[lyr, :]).astype(_BF16)
    merged = gate_a * _wdot(pm, wt.w_pa[...])
    yield

    mu = jnp.mean(v, axis=-1, keepdims=True)
    vc = v - mu
    vn = (vc * lax.rsqrt(jnp.mean(vc * vc, axis=-1, keepdims=True) + EPS) * sm.ln_g[lyr, :]
          + sm.ln_b[lyr, :]).astype(_BF16)
    yield
    u = _gelu_tanh(proj(d, 2 * d))
    yield
    hd = d // SGU_GROUPS
    n_chunks = tm // SGU_CHUNK
    causal = (lax.broadcasted_iota(jnp.int32, (SGU_CHUNK, SGU_CHUNK), 0)
              >= lax.broadcasted_iota(jnp.int32, (SGU_CHUNK, SGU_CHUNK), 1))
    blocks = [[] for _ in range(n_chunks)]
    for g in range(SGU_GROUPS):
        ws = jnp.where(causal, w_s_ref[g], 0.0).astype(_BF16)
        rhs = jnp.concatenate([vn[c * SGU_CHUNK:(c + 1) * SGU_CHUNK, g * hd:(g + 1) * hd]
                               for c in range(n_chunks)], axis=1)
        out = _dot(ws, rhs)
        for c in range(n_chunks):
            blocks[c].append(out[:, c * hd:(c + 1) * hd])
    spatial = jnp.concatenate([jnp.concatenate(b, axis=1) + b_s_ref[...] for b in blocks], axis=0)
    yield
    gate_b = _sigmoid(proj(4 * d, 5 * d))
    yield
    wt.ready("w_pb")
    merged = merged + gate_b * _wdot((u * spatial).astype(_BF16), wt.w_pb[...])
    yield
    wt.ready("w_o")
    o = _wdot(merged.astype(_BF16), wt.w_o[...])
    yield
    x = x + _rms(o, sm.post_mix_g[lyr, :])
    h = _rms(x, sm.pre_ffn_g[lyr, :]).astype(_BF16)
    yield
    f = None
    for c in range(0, FRONT_FF_CHUNKS * FF_CHUNK, FF_CHUNK):
        f = _ffn_chunk(h, f, c, wt)
        yield
    return x, h, f


def _back_stages(x, h, f, p_bf16, layer, sm, wt):
    lyr = slice(layer, layer + 1)
    for c in range(FRONT_FF_CHUNKS * FF_CHUNK, wt.w_ff1.shape[1], FF_CHUNK):
        f = _ffn_chunk(h, f, c, wt)
        yield
    x = x + _rms(f, sm.post_ffn_g[lyr, :])
    yield
    wt.ready("w_proj")
    gate = _sigmoid(_wdot(x.astype(_BF16), wt.w_gate[...]))
    e = _wdot(p_bf16, wt.w_proj[...])
    yield
    return x + _rms(gate * e, sm.post_ple_g[lyr, :])


def _layer_kernel(*refs, layer, tiles_per_seq, n_tiles):
    refs = list(refs)
    x_ref, p_ref, p_prev_ref = refs[:3]
    sm = SimpleNamespace(**dict(zip(_SMALL, refs[3:])))
    k = 3 + len(_SMALL)
    w_s_ref, b_s_ref = refs[k:k + 2]
    hbm = refs[k + 2:k + 2 + len(_BIG)]
    out_hbm = refs[k + 2 + len(_BIG)]
    packed = refs[k + 3 + len(_BIG):k + 3 + 2 * len(_BIG)]
    resident = SimpleNamespace(**dict(zip(_BIG, packed)), ready=lambda *tag: None)
    zbuf_ref, carry_x, carry_h, carry_f, stage_ref, osem_ref, wsem_ref = refs[k + 3 + 2 * len(_BIG):]

    tm, d = x_ref.shape
    assert stage_ref.shape == (4, SUB_TILE, d)
    step = pl.program_id(0)
    j = step % tiles_per_seq
    par = step % 2

    def out_slot(parity, which):
        return 2 * parity + which

    def out_copy(parity, which, sub_tile):
        row = pl.multiple_of(sub_tile * SUB_TILE, SUB_TILE)
        slot = out_slot(parity, which)
        return pltpu.make_async_copy(stage_ref.at[slot], out_hbm.at[pl.ds(row, SUB_TILE), :],
                                     osem_ref.at[slot])

    def weight_loader():
        loader = _WeightLoader(stage_ref, wsem_ref)
        src = {name: ref.at[layer] for name, ref in zip(_BIG, hbm)}
        dst = dict(zip(_BIG, packed))

        def add(tag, name, rows=None, cols=None, index=None):
            s, t = (src[name], dst[name]) if index is None else (src[name].at[index], dst[name].at[index])
            rows = rows or slice(0, s.shape[0])
            cols = cols or slice(0, s.shape[1])
            loader.add(tag, _block_pieces(s, t, rows, cols))

        def in_cols(block):
            add(("w_in", block * d), "w_in", cols=slice(block * d, (block + 1) * d))

        in_cols(0), in_cols(2), in_cols(3)
        for g in range(len(POOL_WINDOWS)):
            add(("pool_w", g), "pool_w", index=g)
        add(("w_pa",), "w_pa")
        in_cols(1), in_cols(4)
        add(("w_pb",), "w_pb")
        add(("w_o",), "w_o")
        for c in range(0, src["w_ff1"].shape[1], FF_CHUNK):
            add(("w_ff1", c), "w_ff1", cols=slice(c, c + FF_CHUNK))
            add(("w_ff", c), "w_ff2", rows=slice(c, c + FF_CHUNK))
        add(("w_gate",), "w_gate")
        add(("w_proj",), "w_proj")
        return loader

    @pl.when(step >= 2)
    def _():
        out_copy(par, 1, 2 * (step - 2)).wait()

    @pl.when(step >= 3)
    def _():
        out_copy(par, 0, 2 * (step - 2) - 1).wait()

    def carried_program(wt):
        state = (carry_x[...], carry_h[...], carry_f[...], p_prev_ref[...].astype(_BF16))
        yield
        stage_ref[out_slot(par, 0)] = yield from _back_stages(*state, layer, sm, wt)

    def lead_program(wt):
        x, h, f = yield from _front_stages(0, j * tm, layer, x_ref, sm, w_s_ref, b_s_ref, wt, zbuf_ref)
        p_bf16 = p_ref[0:SUB_TILE, :].astype(_BF16)
        stage_ref[out_slot(par, 1)] = yield from _back_stages(x, h, f, p_bf16, layer, sm, wt)

    def trail_program(wt):
        x, h, f = yield from _front_stages(SUB_TILE, j * tm, layer, x_ref, sm, w_s_ref, b_s_ref,
                                           wt, zbuf_ref)
        carry_x[...] = x
        carry_h[...] = h
        carry_f[...] = f

    @pl.when(step == 0)
    def _():
        loader = weight_loader()
        arriving = SimpleNamespace(**dict(zip(_BIG, packed)), ready=loader.ready)
        loader.start()
        zbuf_ref[0:HALO, :] = jnp.zeros((HALO, d), _F32)
        _emit_interleaved([(lead_program(arriving), 0, 1), (trail_program(arriving), 4, 1)],
                          after_tick=functools.partial(loader.pump, 3))
        loader.finish()
        out_copy(par, 1, 2 * step).start()

    @pl.when((step >= 1) & (step < n_tiles))
    def _():
        @pl.when(j == 0)
        def _():
            zbuf_ref[0:HALO, :] = jnp.zeros((HALO, d), _F32)

        @pl.when(j != 0)
        def _():
            zbuf_ref[0:HALO, :] = zbuf_ref[tm:tm + HALO, :]

        _emit_interleaved([(carried_program(resident), 0, 2), (lead_program(resident), 0, 1),
                           (trail_program(resident), 4, 1)])
        out_copy(par, 1, 2 * step).start()
        out_copy(par, 0, 2 * step - 1).start()

    @pl.when(step == n_tiles)
    def _():
        _emit_interleaved([(carried_program(resident), 0, 1)])
        out_copy(par, 0, 2 * step - 1).start()
        out_copy(par, 0, 2 * step - 1).wait()
        out_copy(1 - par, 1, 2 * (step - 1)).wait()
        out_copy(1 - par, 0, 2 * (step - 1) - 1).wait()


def _whole(a):
    nd = a.ndim
    return pl.BlockSpec(a.shape, lambda i: (0,) * nd, pipeline_mode=pl.Buffered(1))


def _of_layer(a, layer):
    nd = a.ndim
    return pl.BlockSpec((None,) + a.shape[1:], lambda i: (layer,) + (0,) * (nd - 1),
                        pipeline_mode=pl.Buffered(1))


def _layer_call(x, p, layer, seq, small, w_s, b_s_tile, big):
    n, d = x.shape
    tm = ROW_TILE
    n_tiles = n // tm
    assert n_tiles >= 2
    last = n_tiles - 1
    row = pl.BlockSpec((tm, d), lambda i: (jnp.minimum(i, last), 0))
    prow = pl.BlockSpec((None, tm, p.shape[2]), lambda i: (layer, jnp.minimum(i, last), 0))
    prow_prev = pl.BlockSpec((None, SUB_TILE, p.shape[2]),
                             lambda i: (layer, jnp.maximum(2 * i - 1, 0), 0))
    packed = [pltpu.VMEM(w.shape[1:-2] + (w.shape[-2] // 2, w.shape[-1]), _U32) for w in big]
    return pl.pallas_call(
        functools.partial(_layer_kernel, layer=layer, tiles_per_seq=seq // tm, n_tiles=n_tiles),
        out_shape=jax.ShapeDtypeStruct((n, d), _F32),
        grid=(n_tiles + 1,),
        in_specs=([row, prow, prow_prev] + [_whole(a) for a in small]
                  + [_of_layer(w_s, layer), _of_layer(b_s_tile, layer)]
                  + [pl.BlockSpec(memory_space=pl.ANY)] * len(big)),
        out_specs=pl.BlockSpec(memory_space=pl.ANY),
        scratch_shapes=packed + [
            pltpu.VMEM((HALO + tm, d), _F32),
            pltpu.VMEM((SUB_TILE, d), _F32),
            pltpu.VMEM((SUB_TILE, d), _BF16),
            pltpu.VMEM((SUB_TILE, d), _F32),
            pltpu.VMEM((STAGE_SLOTS, STAGE_ROWS, STAGE_COLS), _F32),
            pltpu.SemaphoreType.DMA((STAGE_SLOTS,)),
            pltpu.SemaphoreType.DMA((STAGE_SLOTS,))],
        compiler_params=pltpu.CompilerParams(
            dimension_semantics=("arbitrary",), vmem_limit_bytes=VMEM_LIMIT_BYTES),
        name="layer",
    )(x, p, p, *small, w_s, b_s_tile, *big)


def kernel(x, p, pre_mix_g, w_in, b_in, pool_w, pool_scale, sgu_ln_g, sgu_ln_b, sgu_w_s,
           sgu_b_s, w_pa, w_pb, w_o, post_mix_g, pre_ffn_g, w_ff1, w_ff2, post_ffn_g,
           w_ple_gate, w_ple_proj, post_ple_g):
    b, s, d = x.shape
    depth = w_in.shape[0]
    assert s % ROW_TILE == 0 and SUB_TILE % SGU_CHUNK == 0
    assert d % SGU_GROUPS == 0 and (d // len(POOL_WINDOWS)) % 128 == 0
    assert w_ff1.shape[2] % FF_CHUNK == 0
    xf = x.reshape(b * s, d)
    pf = p.reshape(depth, b * s, p.shape[-1])
    b_s_tile = jnp.repeat(jnp.swapaxes(sgu_b_s, 1, 2), d // SGU_GROUPS, axis=2)
    small = (pre_mix_g, b_in, pool_scale, sgu_ln_g, sgu_ln_b, post_mix_g,
             pre_ffn_g, post_ffn_g, post_ple_g)
    big = (w_in, pool_w, w_pa, w_pb, w_o, w_ff1, w_ff2, w_ple_gate, w_ple_proj)
    for i in range(depth):
        xf = _layer_call(xf, pf, i, s, small, sgu_w_s, b_s_tile, big)
    return xf.reshape(b, s, d)
```

```python
import functools
from types import SimpleNamespace

import jax
import jax.numpy as jnp
from jax import lax
from jax.experimental import pallas as pl
from jax.experimental.pallas import tpu as pltpu

POOL_WINDOWS = (2, 4, 8, 16)
SUBLANES = 8
HALO = 16
SGU_CHUNK = 128
SGU_GROUPS = 8
EPS = 1e-6
FF_CHUNK = 1024
FRONT_FF_CHUNKS = 2
SUB_TILE = 256
ROW_TILE = 2 * SUB_TILE
STAGE_ROWS, STAGE_COLS = SUB_TILE, 1024
STAGE_SLOTS = 4
VMEM_LIMIT_BYTES = 60 * 1024 * 1024

_BF16 = jnp.bfloat16
_F32 = jnp.float32
_U32 = jnp.uint32

_SMALL = ("pre_mix_g", "b_in", "pool_scale", "ln_g", "ln_b", "post_mix_g",
          "pre_ffn_g", "post_ffn_g", "post_ple_g")
_BIG = ("w_in", "pool_w", "w_pa", "w_pb", "w_o", "w_ff1", "w_ff2", "w_gate", "w_proj")


def _rms(x, g):
    return x * lax.rsqrt(jnp.mean(x * x, axis=-1, keepdims=True) + EPS) * g


def _dot(a, b):
    return jnp.dot(a, b, preferred_element_type=_F32)


def _wdot(a, w_packed):
    return _dot(a, pltpu.bitcast(w_packed, _BF16))


_LOG2E = 1.4426950408889634
_GELU_C1 = -2.0 * 0.7978845608028654 * _LOG2E
_GELU_C2 = _GELU_C1 * 0.044715


def _sigmoid(x):
    return 1.0 / (1.0 + jnp.exp2(x * -_LOG2E))


def _gelu_tanh(x):
    return x / (1.0 + jnp.exp2(x * (_GELU_C1 + _GELU_C2 * (x * x))))


def _matrix_pieces(src, dst):
    k, n = src.shape
    pieces = []
    for r in range(0, k, STAGE_ROWS):
        nr = min(STAGE_ROWS, k - r)
        for c in range(0, n, STAGE_COLS):
            nc = min(STAGE_COLS, n - c)
            pieces.append((src.at[pl.ds(r, nr), pl.ds(c, nc)],
                           dst.at[pl.ds(r // 2, nr // 2), pl.ds(c, nc)]))
    return pieces


def _load_weights(pieces, stage_ref, sem_ref):
    slots = stage_ref.shape[0]

    def copy(i):
        src = pieces[i][0]
        nr, nc = src.shape
        return pltpu.make_async_copy(
            src, stage_ref.at[i % slots, pl.ds(0, nr), pl.ds(0, nc)], sem_ref.at[i % slots])

    for i in range(min(slots, len(pieces))):
        copy(i).start()
    for i, (src, dst) in enumerate(pieces):
        nr, nc = src.shape
        copy(i).wait()
        dst[...] = pltpu.bitcast(stage_ref[i % slots, 0:nr, 0:nc].astype(_BF16), _U32)
        if i + slots < len(pieces):
            copy(i + slots).start()


def _emit_interleaved(schedule):
    live = list(schedule)
    tick = 0
    while live:
        for item in list(live):
            prog, first, period = item
            if tick >= first and (tick - first) % period == 0:
                if next(prog, StopIteration) is StopIteration:
                    live.remove(item)
        tick += 1


def _window_sums(zext, w):
    s = zext
    k = 1
    while k < min(w, SUBLANES):
        s = s + pltpu.roll(s, k, axis=0)
        k *= 2
    win = s[HALO:, :]
    if w > SUBLANES:
        assert w == 2 * SUBLANES
        win = win + s[HALO - SUBLANES:-SUBLANES, :]
    return win


def _ffn_chunk(h, f, c, wt):
    a = _wdot(h, wt.w_ff1[:, c:c + FF_CHUNK])
    a = jnp.square(jnp.maximum(a, 0.0)).astype(_BF16)
    part = _wdot(a, wt.w_ff2[c // 2:(c + FF_CHUNK) // 2, :])
    return part if f is None else f + part


def _front_stages(row0, seq_row0, layer, x_ref, sm, w_s_ref, b_s_ref, wt, zbuf_ref):
    tm = SUB_TILE
    d = x_ref.shape[1]
    rows = slice(row0, row0 + tm)
    zrows = slice(HALO + row0, HALO + row0 + tm)
    lyr = slice(layer, layer + 1)

    x = x_ref[rows, :]
    h = _rms(x, sm.pre_mix_g[lyr, :]).astype(_BF16)
    yield

    def proj(lo, hi):
        return _wdot(h, wt.w_in[:, lo:hi]) + sm.b_in[lyr, lo:hi]

    zbuf_ref[zrows, :] = proj(0, d)
    yield
    v = _gelu_tanh(proj(2 * d, 3 * d))
    yield

    t = seq_row0 + row0 + lax.broadcasted_iota(jnp.int32, (tm, 128), 0)
    gdim = d // len(POOL_WINDOWS)
    pooled = []
    for k, w in enumerate(POOL_WINDOWS):
        zext = zbuf_ref[row0:row0 + HALO + tm, k * gdim:(k + 1) * gdim]
        win = _window_sums(zext, w)
        inv_cnt = 1.0 / jnp.minimum(t + 1, w).astype(_F32)
        inv_cnt = jnp.concatenate([inv_cnt] * (gdim // 128), axis=1)
        pooled.append((win * inv_cnt - zext[HALO:, :]).astype(_BF16))
    yield
    gate_a = _sigmoid(proj(3 * d, 4 * d))
    yield
    pm = jnp.concatenate([_wdot(pooled[k], wt.pool_w[k]) for k in range(len(POOL_WINDOWS))], axis=-1)
    pm = (pm * sm.pool_scale[lyr, :]).astype(_BF16)
    merged = gate_a * _wdot(pm, wt.w_pa[...])
    yield

    mu = jnp.mean(v, axis=-1, keepdims=True)
    vc = v - mu
    vn = (vc * lax.rsqrt(jnp.mean(vc * vc, axis=-1, keepdims=True) + EPS) * sm.ln_g[lyr, :]
          + sm.ln_b[lyr, :]).astype(_BF16)
    yield
    u = _gelu_tanh(proj(d, 2 * d))
    yield
    hd = d // SGU_GROUPS
    n_chunks = tm // SGU_CHUNK
    causal = (lax.broadcasted_iota(jnp.int32, (SGU_CHUNK, SGU_CHUNK), 0)
              >= lax.broadcasted_iota(jnp.int32, (SGU_CHUNK, SGU_CHUNK), 1))
    blocks = [[] for _ in range(n_chunks)]
    for g in range(SGU_GROUPS):
        ws = jnp.where(causal, w_s_ref[g], 0.0).astype(_BF16)
        rhs = jnp.concatenate([vn[c * SGU_CHUNK:(c + 1) * SGU_CHUNK, g * hd:(g + 1) * hd]
                               for c in range(n_chunks)], axis=1)
        out = _dot(ws, rhs)
        for c in range(n_chunks):
            blocks[c].append(out[:, c * hd:(c + 1) * hd])
    spatial = jnp.concatenate([jnp.concatenate(b, axis=1) + b_s_ref[...] for b in blocks], axis=0)
    yield
    gate_b = _sigmoid(proj(4 * d, 5 * d))
    yield
    merged = merged + gate_b * _wdot((u * spatial).astype(_BF16), wt.w_pb[...])
    yield
    o = _wdot(merged.astype(_BF16), wt.w_o[...])
    yield
    x = x + _rms(o, sm.post_mix_g[lyr, :])
    h = _rms(x, sm.pre_ffn_g[lyr, :]).astype(_BF16)
    yield
    f = None
    for c in range(0, FRONT_FF_CHUNKS * FF_CHUNK, FF_CHUNK):
        f = _ffn_chunk(h, f, c, wt)
        yield
    return x, h, f


def _back_stages(x, h, f, p_bf16, layer, sm, wt):
    lyr = slice(layer, layer + 1)
    for c in range(FRONT_FF_CHUNKS * FF_CHUNK, wt.w_ff1.shape[1], FF_CHUNK):
        f = _ffn_chunk(h, f, c, wt)
        yield
    x = x + _rms(f, sm.post_ffn_g[lyr, :])
    yield
    gate = _sigmoid(_wdot(x.astype(_BF16), wt.w_gate[...]))
    e = _wdot(p_bf16, wt.w_proj[...])
    yield
    return x + _rms(gate * e, sm.post_ple_g[lyr, :])


def _layer_kernel(*refs, layer, tiles_per_seq, n_tiles):
    refs = list(refs)
    x_ref, p_ref, p_prev_ref = refs[:3]
    sm = SimpleNamespace(**dict(zip(_SMALL, refs[3:])))
    k = 3 + len(_SMALL)
    w_s_ref, b_s_ref = refs[k:k + 2]
    hbm = refs[k + 2:k + 2 + len(_BIG)]
    out_hbm = refs[k + 2 + len(_BIG)]
    packed = refs[k + 3 + len(_BIG):k + 3 + 2 * len(_BIG)]
    wt = SimpleNamespace(**dict(zip(_BIG, packed)))
    zbuf_ref, carry_x, carry_h, carry_f, stage_ref, osem_ref, wsem_ref = refs[k + 3 + 2 * len(_BIG):]

    tm, d = x_ref.shape
    assert stage_ref.shape == (4, SUB_TILE, d)
    step = pl.program_id(0)
    j = step % tiles_per_seq
    par = step % 2

    def out_slot(parity, which):
        return 2 * parity + which

    def out_copy(parity, which, sub_tile):
        row = pl.multiple_of(sub_tile * SUB_TILE, SUB_TILE)
        slot = out_slot(parity, which)
        return pltpu.make_async_copy(stage_ref.at[slot], out_hbm.at[pl.ds(row, SUB_TILE), :],
                                     osem_ref.at[slot])

    @pl.when(step == 0)
    def _():
        pieces = []
        for name, src, dst in zip(_BIG, hbm, packed):
            if name == "pool_w":
                for g in range(len(POOL_WINDOWS)):
                    pieces += _matrix_pieces(src.at[layer, g], dst.at[g])
            else:
                pieces += _matrix_pieces(src.at[layer], dst)
        _load_weights(pieces, stage_ref, wsem_ref)
        carry_x[...] = jnp.zeros(carry_x.shape, carry_x.dtype)
        carry_h[...] = jnp.zeros(carry_h.shape, carry_h.dtype)
        carry_f[...] = jnp.zeros(carry_f.shape, carry_f.dtype)

    @pl.when(step >= 2)
    def _():
        out_copy(par, 1, 2 * (step - 2)).wait()

    @pl.when(step >= 3)
    def _():
        out_copy(par, 0, 2 * (step - 2) - 1).wait()

    def carried_program():
        state = (carry_x[...], carry_h[...], carry_f[...], p_prev_ref[...].astype(_BF16))
        stage_ref[out_slot(par, 0)] = yield from _back_stages(*state, layer, sm, wt)

    def lead_program():
        x, h, f = yield from _front_stages(0, j * tm, layer, x_ref, sm, w_s_ref, b_s_ref, wt, zbuf_ref)
        p_bf16 = p_ref[0:SUB_TILE, :].astype(_BF16)
        stage_ref[out_slot(par, 1)] = yield from _back_stages(x, h, f, p_bf16, layer, sm, wt)

    def trail_program():
        x, h, f = yield from _front_stages(SUB_TILE, j * tm, layer, x_ref, sm, w_s_ref, b_s_ref,
                                           wt, zbuf_ref)
        carry_x[...] = x
        carry_h[...] = h
        carry_f[...] = f

    @pl.when(step < n_tiles)
    def _():
        @pl.when(j == 0)
        def _():
            zbuf_ref[0:HALO, :] = jnp.zeros((HALO, d), _F32)

        @pl.when(j != 0)
        def _():
            zbuf_ref[0:HALO, :] = zbuf_ref[tm:tm + HALO, :]

        _emit_interleaved([(carried_program(), 0, 2), (lead_program(), 0, 1), (trail_program(), 4, 1)])
        out_copy(par, 1, 2 * step).start()

        @pl.when(step >= 1)
        def _():
            out_copy(par, 0, 2 * step - 1).start()

    @pl.when(step == n_tiles)
    def _():
        _emit_interleaved([(carried_program(), 0, 1)])
        out_copy(par, 0, 2 * step - 1).start()
        out_copy(par, 0, 2 * step - 1).wait()
        out_copy(1 - par, 1, 2 * (step - 1)).wait()
        out_copy(1 - par, 0, 2 * (step - 1) - 1).wait()


def _whole(a):
    nd = a.ndim
    return pl.BlockSpec(a.shape, lambda i: (0,) * nd, pipeline_mode=pl.Buffered(1))


def _of_layer(a, layer):
    nd = a.ndim
    return pl.BlockSpec((None,) + a.shape[1:], lambda i: (layer,) + (0,) * (nd - 1),
                        pipeline_mode=pl.Buffered(1))


def _layer_call(x, p, layer, seq, small, w_s, b_s_tile, big):
    n, d = x.shape
    tm = ROW_TILE
    n_tiles = n // tm
    assert n_tiles >= 2
    last = n_tiles - 1
    row = pl.BlockSpec((tm, d), lambda i: (jnp.minimum(i, last), 0))
    prow = pl.BlockSpec((None, tm, p.shape[2]), lambda i: (layer, jnp.minimum(i, last), 0))
    prow_prev = pl.BlockSpec((None, SUB_TILE, p.shape[2]),
                             lambda i: (layer, jnp.maximum(2 * i - 1, 0), 0))
    packed = [pltpu.VMEM(w.shape[1:-2] + (w.shape[-2] // 2, w.shape[-1]), _U32) for w in big]
    return pl.pallas_call(
        functools.partial(_layer_kernel, layer=layer, tiles_per_seq=seq // tm, n_tiles=n_tiles),
        out_shape=jax.ShapeDtypeStruct((n, d), _F32),
        grid=(n_tiles + 1,),
        in_specs=([row, prow, prow_prev] + [_whole(a) for a in small]
                  + [_of_layer(w_s, layer), _of_layer(b_s_tile, layer)]
                  + [pl.BlockSpec(memory_space=pl.ANY)] * len(big)),
        out_specs=pl.BlockSpec(memory_space=pl.ANY),
        scratch_shapes=packed + [
            pltpu.VMEM((HALO + tm, d), _F32),
            pltpu.VMEM((SUB_TILE, d), _F32),
            pltpu.VMEM((SUB_TILE, d), _BF16),
            pltpu.VMEM((SUB_TILE, d), _F32),
            pltpu.VMEM((STAGE_SLOTS, STAGE_ROWS, STAGE_COLS), _F32),
            pltpu.SemaphoreType.DMA((STAGE_SLOTS,)),
            pltpu.SemaphoreType.DMA((STAGE_SLOTS,))],
        compiler_params=pltpu.CompilerParams(
            dimension_semantics=("arbitrary",), vmem_limit_bytes=VMEM_LIMIT_BYTES),
        name="layer",
    )(x, p, p, *small, w_s, b_s_tile, *big)


def kernel(x, p, pre_mix_g, w_in, b_in, pool_w, pool_scale, sgu_ln_g, sgu_ln_b, sgu_w_s,
           sgu_b_s, w_pa, w_pb, w_o, post_mix_g, pre_ffn_g, w_ff1, w_ff2, post_ffn_g,
           w_ple_gate, w_ple_proj, post_ple_g):
    b, s, d = x.shape
    depth = w_in.shape[0]
    assert s % ROW_TILE == 0 and SUB_TILE % SGU_CHUNK == 0
    assert d % SGU_GROUPS == 0 and (d // len(POOL_WINDOWS)) % 128 == 0
    assert w_ff1.shape[2] % FF_CHUNK == 0
    xf = x.reshape(b * s, d)
    pf = p.reshape(depth, b * s, p.shape[-1])
    b_s_tile = jnp.repeat(jnp.swapaxes(sgu_b_s, 1, 2), d // SGU_GROUPS, axis=2)
    small = (pre_mix_g, b_in, pool_scale, sgu_ln_g, sgu_ln_b, post_mix_g,
             pre_ffn_g, post_ffn_g, post_ple_g)
    big = (w_in, pool_w, w_pa, w_pb, w_o, w_ff1, w_ff2, w_ple_gate, w_ple_proj)
    for i in range(depth):
        xf = _layer_call(xf, pf, i, s, small, sgu_w_s, b_s_tile, big)
    return xf.reshape(b, s, d)
```

```python
import functools
from types import SimpleNamespace

import jax
import jax.numpy as jnp
from jax import lax
from jax.experimental import pallas as pl
from jax.experimental.pallas import tpu as pltpu

POOL_WINDOWS = (2, 4, 8, 16)
SUBLANES, LANES = 8, 128
HALO = 16
SGU_CHUNK = 128
SGU_GROUPS = 8
EPS = 1e-6
FF_CHUNK = 1024
FRONT_FF_CHUNKS = 2
SUB_TILE = 256
ROW_TILE = 2 * SUB_TILE
STAGE_ROWS, STAGE_COLS = SUB_TILE, 1024
STAGE_SLOTS = 4
VMEM_LIMIT_BYTES = 60 * 1024 * 1024

_BF16 = jnp.bfloat16
_F32 = jnp.float32
_U32 = jnp.uint32

_SMALL = ("pre_mix_g", "b_in", "pool_scale", "ln_g", "ln_b", "post_mix_g",
          "pre_ffn_g", "post_ffn_g", "post_ple_g")
_BIG = ("w_in", "pool_w", "w_pa", "w_pb", "w_o", "w_ff1", "w_ff2", "w_gate", "w_proj")


def _rms(x, g):
    return x * lax.rsqrt(jnp.mean(x * x, axis=-1, keepdims=True) + EPS) * g


def _dot(a, b):
    return jnp.dot(a, b, preferred_element_type=_F32)


def _wdot(a, w_packed):
    return _dot(a, pltpu.bitcast(w_packed, _BF16))


_LOG2E = 1.4426950408889634
_GELU_C1 = -2.0 * 0.7978845608028654 * _LOG2E
_GELU_C2 = _GELU_C1 * 0.044715


def _sigmoid(x):
    return 1.0 / (1.0 + jnp.exp2(x * -_LOG2E))


def _gelu_tanh(x):
    return x / (1.0 + jnp.exp2(x * (_GELU_C1 + _GELU_C2 * (x * x))))


def _matrix_pieces(src, dst):
    k, n = src.shape
    pieces = []
    for r in range(0, k, STAGE_ROWS):
        nr = min(STAGE_ROWS, k - r)
        for c in range(0, n, STAGE_COLS):
            nc = min(STAGE_COLS, n - c)
            pieces.append((src.at[pl.ds(r, nr), pl.ds(c, nc)],
                           dst.at[pl.ds(r // 2, nr // 2), pl.ds(c, nc)]))
    return pieces


def _load_weights(pieces, stage_ref, sem_ref):
    slots = stage_ref.shape[0]

    def copy(i):
        src = pieces[i][0]
        nr, nc = src.shape
        return pltpu.make_async_copy(
            src, stage_ref.at[i % slots, pl.ds(0, nr), pl.ds(0, nc)], sem_ref.at[i % slots])

    for i in range(min(slots, len(pieces))):
        copy(i).start()
    for i, (src, dst) in enumerate(pieces):
        nr, nc = src.shape
        copy(i).wait()
        dst[...] = pltpu.bitcast(stage_ref[i % slots, 0:nr, 0:nc].astype(_BF16), _U32)
        if i + slots < len(pieces):
            copy(i + slots).start()


def _emit_interleaved(schedule):
    live = list(schedule)
    tick = 0
    while live:
        for item in list(live):
            prog, first, period = item
            if tick >= first and (tick - first) % period == 0:
                if next(prog, StopIteration) is StopIteration:
                    live.remove(item)
        tick += 1


def _window_sums(zext, w):
    s = zext
    k = 1
    while k < min(w, SUBLANES):
        s = s + pltpu.roll(s, k, axis=0)
        k *= 2
    win = s[HALO:, :]
    if w > SUBLANES:
        assert w == 2 * SUBLANES
        win = win + s[HALO - SUBLANES:-SUBLANES, :]
    return win


def _ffn_chunk(h, f, c, wt):
    a = _wdot(h, wt.w_ff1[:, c:c + FF_CHUNK])
    a = jnp.square(jnp.maximum(a, 0.0)).astype(_BF16)
    part = _wdot(a, wt.w_ff2[c // 2:(c + FF_CHUNK) // 2, :])
    return part if f is None else f + part


def _front_stages(row0, seq_row0, layer, x_ref, sm, w_s_ref, b_s_ref, wt, zbuf_ref):
    tm = SUB_TILE
    d = x_ref.shape[1]
    rows = slice(row0, row0 + tm)
    zrows = slice(HALO + row0, HALO + row0 + tm)
    lyr = slice(layer, layer + 1)

    x = x_ref[rows, :]
    h = _rms(x, sm.pre_mix_g[lyr, :]).astype(_BF16)
    yield

    def proj(lo, hi):
        return _wdot(h, wt.w_in[:, lo:hi]) + sm.b_in[lyr, lo:hi]

    zbuf_ref[zrows, :] = proj(0, d)
    yield
    v = _gelu_tanh(proj(2 * d, 3 * d))
    yield

    t = seq_row0 + row0 + lax.broadcasted_iota(jnp.int32, (tm, LANES), 0)
    gdim = d // len(POOL_WINDOWS)
    pooled = []
    for k, w in enumerate(POOL_WINDOWS):
        zext = zbuf_ref[row0:row0 + HALO + tm, k * gdim:(k + 1) * gdim]
        win = _window_sums(zext, w)
        inv_cnt = 1.0 / jnp.minimum(t + 1, w).astype(_F32)
        inv_cnt = jnp.concatenate([inv_cnt] * (gdim // LANES), axis=1)
        pooled.append((win * inv_cnt - zext[HALO:, :]).astype(_BF16))
    yield
    gate_a = _sigmoid(proj(3 * d, 4 * d))
    yield
    pm = jnp.concatenate([_wdot(pooled[k], wt.pool_w[k]) for k in range(len(POOL_WINDOWS))], axis=-1)
    pm = (pm * sm.pool_scale[lyr, :]).astype(_BF16)
    merged = gate_a * _wdot(pm, wt.w_pa[...])
    yield

    mu = jnp.mean(v, axis=-1, keepdims=True)
    vc = v - mu
    vn = (vc * lax.rsqrt(jnp.mean(vc * vc, axis=-1, keepdims=True) + EPS) * sm.ln_g[lyr, :]
          + sm.ln_b[lyr, :]).astype(_BF16)
    yield
    u = _gelu_tanh(proj(d, 2 * d))
    yield
    hd = d // SGU_GROUPS
    n_chunks = tm // SGU_CHUNK
    causal = (lax.broadcasted_iota(jnp.int32, (SGU_CHUNK, SGU_CHUNK), 0)
              >= lax.broadcasted_iota(jnp.int32, (SGU_CHUNK, SGU_CHUNK), 1))
    blocks = [[] for _ in range(n_chunks)]
    for g in range(SGU_GROUPS):
        ws = jnp.where(causal, w_s_ref[g], 0.0).astype(_BF16)
        rhs = jnp.concatenate([vn[c * SGU_CHUNK:(c + 1) * SGU_CHUNK, g * hd:(g + 1) * hd]
                               for c in range(n_chunks)], axis=1)
        out = _dot(ws, rhs)
        for c in range(n_chunks):
            blocks[c].append(out[:, c * hd:(c + 1) * hd])
    spatial = jnp.concatenate([jnp.concatenate(b, axis=1) + b_s_ref[...] for b in blocks], axis=0)
    yield
    gate_b = _sigmoid(proj(4 * d, 5 * d))
    yield
    merged = merged + gate_b * _wdot((u * spatial).astype(_BF16), wt.w_pb[...])
    yield
    o = _wdot(merged.astype(_BF16), wt.w_o[...])
    yield
    x = x + _rms(o, sm.post_mix_g[lyr, :])
    h = _rms(x, sm.pre_ffn_g[lyr, :]).astype(_BF16)
    yield
    f = None
    for c in range(0, FRONT_FF_CHUNKS * FF_CHUNK, FF_CHUNK):
        f = _ffn_chunk(h, f, c, wt)
        yield
    return x, h, f


def _back_stages(x, h, f, p_bf16, layer, sm, wt):
    lyr = slice(layer, layer + 1)
    for c in range(FRONT_FF_CHUNKS * FF_CHUNK, wt.w_ff1.shape[1], FF_CHUNK):
        f = _ffn_chunk(h, f, c, wt)
        yield
    x = x + _rms(f, sm.post_ffn_g[lyr, :])
    yield
    gate = _sigmoid(_wdot(x.astype(_BF16), wt.w_gate[...]))
    e = _wdot(p_bf16, wt.w_proj[...])
    yield
    return x + _rms(gate * e, sm.post_ple_g[lyr, :])


def _layer_kernel(*refs, layer, tiles_per_seq, n_tiles):
    refs = list(refs)
    x_ref, p_ref, p_prev_ref = refs[:3]
    sm = SimpleNamespace(**dict(zip(_SMALL, refs[3:])))
    k = 3 + len(_SMALL)
    w_s_ref, b_s_ref = refs[k:k + 2]
    hbm = refs[k + 2:k + 2 + len(_BIG)]
    out_hbm = refs[k + 2 + len(_BIG)]
    packed = refs[k + 3 + len(_BIG):k + 3 + 2 * len(_BIG)]
    wt = SimpleNamespace(**dict(zip(_BIG, packed)))
    zbuf_ref, carry_x, carry_h, carry_f, stage_ref, osem_ref, wsem_ref = refs[k + 3 + 2 * len(_BIG):]

    tm, d = x_ref.shape
    assert stage_ref.shape == (STAGE_SLOTS, SUB_TILE, d) and STAGE_SLOTS == 4
    step = pl.program_id(0)
    j = step % tiles_per_seq
    par = step % 2

    def out_slot(parity, which):
        return 2 * parity + which

    def out_copy(parity, which, sub_tile):
        row = pl.multiple_of(sub_tile * SUB_TILE, SUB_TILE)
        slot = out_slot(parity, which)
        return pltpu.make_async_copy(stage_ref.at[slot], out_hbm.at[pl.ds(row, SUB_TILE), :],
                                     osem_ref.at[slot])

    @pl.when(step == 0)
    def _():
        pieces = []
        for name, src, dst in zip(_BIG, hbm, packed):
            if name == "pool_w":
                for g in range(len(POOL_WINDOWS)):
                    pieces += _matrix_pieces(src.at[layer, g], dst.at[g])
            else:
                pieces += _matrix_pieces(src.at[layer], dst)
        _load_weights(pieces, stage_ref, wsem_ref)
        carry_x[...] = jnp.zeros(carry_x.shape, carry_x.dtype)
        carry_h[...] = jnp.zeros(carry_h.shape, carry_h.dtype)
        carry_f[...] = jnp.zeros(carry_f.shape, carry_f.dtype)

    @pl.when(step >= 2)
    def _():
        out_copy(par, 1, 2 * (step - 2)).wait()

    @pl.when(step >= 3)
    def _():
        out_copy(par, 0, 2 * (step - 2) - 1).wait()

    def carried_program():
        state = (carry_x[...], carry_h[...], carry_f[...], p_prev_ref[...].astype(_BF16))
        stage_ref[out_slot(par, 0)] = yield from _back_stages(*state, layer, sm, wt)

    def lead_program():
        x, h, f = yield from _front_stages(0, j * tm, layer, x_ref, sm, w_s_ref, b_s_ref, wt, zbuf_ref)
        p_bf16 = p_ref[0:SUB_TILE, :].astype(_BF16)
        stage_ref[out_slot(par, 1)] = yield from _back_stages(x, h, f, p_bf16, layer, sm, wt)

    def trail_program():
        x, h, f = yield from _front_stages(SUB_TILE, j * tm, layer, x_ref, sm, w_s_ref, b_s_ref,
                                           wt, zbuf_ref)
        carry_x[...] = x
        carry_h[...] = h
        carry_f[...] = f

    @pl.when(step < n_tiles)
    def _():
        @pl.when(j == 0)
        def _():
            zbuf_ref[0:HALO, :] = jnp.zeros((HALO, d), _F32)

        @pl.when(j != 0)
        def _():
            zbuf_ref[0:HALO, :] = zbuf_ref[tm:tm + HALO, :]

        _emit_interleaved([(carried_program(), 0, 2), (lead_program(), 0, 1), (trail_program(), 4, 1)])
        out_copy(par, 1, 2 * step).start()

        @pl.when(step >= 1)
        def _():
            out_copy(par, 0, 2 * step - 1).start()

    @pl.when(step == n_tiles)
    def _():
        _emit_interleaved([(carried_program(), 0, 1)])
        out_copy(par, 0, 2 * step - 1).start()
        out_copy(par, 0, 2 * step - 1).wait()
        out_copy(1 - par, 1, 2 * (step - 1)).wait()
        out_copy(1 - par, 0, 2 * (step - 1) - 1).wait()


def _whole(a):
    nd = a.ndim
    return pl.BlockSpec(a.shape, lambda i: (0,) * nd, pipeline_mode=pl.Buffered(1))


def _of_layer(a, layer):
    nd = a.ndim
    return pl.BlockSpec((None,) + a.shape[1:], lambda i: (layer,) + (0,) * (nd - 1),
                        pipeline_mode=pl.Buffered(1))


def _layer_call(x, p, layer, seq, small, w_s, b_s_tile, big):
    n, d = x.shape
    tm = ROW_TILE
    n_tiles = n // tm
    assert n_tiles >= 2
    last = n_tiles - 1
    row = pl.BlockSpec((tm, d), lambda i: (jnp.minimum(i, last), 0))
    prow = pl.BlockSpec((None, tm, p.shape[2]), lambda i: (layer, jnp.minimum(i, last), 0))
    prow_prev = pl.BlockSpec((None, SUB_TILE, p.shape[2]),
                             lambda i: (layer, jnp.maximum(2 * i - 1, 0), 0))
    packed = [pltpu.VMEM(w.shape[1:-2] + (w.shape[-2] // 2, w.shape[-1]), _U32) for w in big]
    return pl.pallas_call(
        functools.partial(_layer_kernel, layer=layer, tiles_per_seq=seq // tm, n_tiles=n_tiles),
        out_shape=jax.ShapeDtypeStruct((n, d), _F32),
        grid=(n_tiles + 1,),
        in_specs=([row, prow, prow_prev] + [_whole(a) for a in small]
                  + [_of_layer(w_s, layer), _of_layer(b_s_tile, layer)]
                  + [pl.BlockSpec(memory_space=pl.ANY)] * len(big)),
        out_specs=pl.BlockSpec(memory_space=pl.ANY),
        scratch_shapes=packed + [
            pltpu.VMEM((HALO + tm, d), _F32),
            pltpu.VMEM((SUB_TILE, d), _F32),
            pltpu.VMEM((SUB_TILE, d), _BF16),
            pltpu.VMEM((SUB_TILE, d), _F32),
            pltpu.VMEM((STAGE_SLOTS, STAGE_ROWS, STAGE_COLS), _F32),
            pltpu.SemaphoreType.DMA((STAGE_SLOTS,)),
            pltpu.SemaphoreType.DMA((STAGE_SLOTS,))],
        compiler_params=pltpu.CompilerParams(
            dimension_semantics=("arbitrary",), vmem_limit_bytes=VMEM_LIMIT_BYTES),
        name="layer",
    )(x, p, p, *small, w_s, b_s_tile, *big)


def kernel(x, p, pre_mix_g, w_in, b_in, pool_w, pool_scale, sgu_ln_g, sgu_ln_b, sgu_w_s,
           sgu_b_s, w_pa, w_pb, w_o, post_mix_g, pre_ffn_g, w_ff1, w_ff2, post_ffn_g,
           w_ple_gate, w_ple_proj, post_ple_g):
    b, s, d = x.shape
    depth = w_in.shape[0]
    assert s % ROW_TILE == 0 and SUB_TILE % SGU_CHUNK == 0
    assert d % SGU_GROUPS == 0 and (d // len(POOL_WINDOWS)) % LANES == 0
    assert w_ff1.shape[2] % FF_CHUNK == 0
    xf = x.reshape(b * s, d)
    pf = p.reshape(depth, b * s, p.shape[-1])
    b_s_tile = jnp.repeat(jnp.swapaxes(sgu_b_s, 1, 2), d // SGU_GROUPS, axis=2)
    small = (pre_mix_g, b_in, pool_scale, sgu_ln_g, sgu_ln_b, post_mix_g,
             pre_ffn_g, post_ffn_g, post_ple_g)
    big = (w_in, pool_w, w_pa, w_pb, w_o, w_ff1, w_ff2, w_ple_gate, w_ple_proj)
    for i in range(depth):
        xf = _layer_call(xf, pf, i, s, small, sgu_w_s, b_s_tile, big)
    return xf.reshape(b, s, d)
```

```python
import functools
from types import SimpleNamespace

import jax
import jax.numpy as jnp
from jax import lax
from jax.experimental import pallas as pl
from jax.experimental.pallas import tpu as pltpu

POOL_WINDOWS = (2, 4, 8, 16)
SUBLANES, LANES = 8, 128
HALO = 16
SGU_CHUNK = 128
SGU_GROUPS = 8
EPS = 1e-6
FF_CHUNK = 1024
FRONT_FF_CHUNKS = 2
SUB_TILE = 256
ROW_TILE = 2 * SUB_TILE
STAGE_ROWS, STAGE_COLS = SUB_TILE, 1024
STAGE_SLOTS = 4
VMEM_LIMIT_BYTES = 60 * 1024 * 1024

_BF16 = jnp.bfloat16
_F32 = jnp.float32
_U32 = jnp.uint32

_SMALL = ("pre_mix_g", "b_in", "pool_scale", "ln_g", "ln_b", "post_mix_g",
          "pre_ffn_g", "post_ffn_g", "post_ple_g")
_BIG = ("w_in", "pool_w", "w_pa", "w_pb", "w_o", "w_ff1", "w_ff2", "w_gate", "w_proj")


def _rms(x, g):
    return x * lax.rsqrt(jnp.mean(x * x, axis=-1, keepdims=True) + EPS) * g


def _dot(a, b):
    return jnp.dot(a, b, preferred_element_type=_F32)


def _wdot(a, w_packed):
    return _dot(a, pltpu.bitcast(w_packed, _BF16))


_LOG2E = 1.4426950408889634
_GELU_C1 = -2.0 * 0.7978845608028654 * _LOG2E
_GELU_C2 = _GELU_C1 * 0.044715


def _sigmoid(x):
    return 1.0 / (1.0 + jnp.exp2(x * -_LOG2E))


def _gelu_tanh(x):
    return x / (1.0 + jnp.exp2(x * (_GELU_C1 + _GELU_C2 * (x * x))))


def _matrix_pieces(src, dst):
    k, n = src.shape
    pieces = []
    for r in range(0, k, STAGE_ROWS):
        nr = min(STAGE_ROWS, k - r)
        for c in range(0, n, STAGE_COLS):
            nc = min(STAGE_COLS, n - c)
            pieces.append((src.at[pl.ds(r, nr), pl.ds(c, nc)],
                           dst.at[pl.ds(r // 2, nr // 2), pl.ds(c, nc)]))
    return pieces


def _load_weights(pieces, stage_ref, sem_ref):
    slots = stage_ref.shape[0]

    def copy(i):
        src = pieces[i][0]
        nr, nc = src.shape
        return pltpu.make_async_copy(
            src, stage_ref.at[i % slots, pl.ds(0, nr), pl.ds(0, nc)], sem_ref.at[i % slots])

    for i in range(min(slots, len(pieces))):
        copy(i).start(priority=i % 2)
    for i, (src, dst) in enumerate(pieces):
        nr, nc = src.shape
        copy(i).wait()
        dst[...] = pltpu.bitcast(stage_ref[i % slots, 0:nr, 0:nc].astype(_BF16), _U32)
        if i + slots < len(pieces):
            copy(i + slots).start(priority=(i + slots) % 2)


def _emit_interleaved(schedule):
    live = list(schedule)
    tick = 0
    while live:
        for item in list(live):
            prog, first, period = item
            if tick >= first and (tick - first) % period == 0:
                if next(prog, StopIteration) is StopIteration:
                    live.remove(item)
        tick += 1


def _window_sums(zext, w):
    s = zext
    k = 1
    while k < min(w, SUBLANES):
        s = s + pltpu.roll(s, k, axis=0)
        k *= 2
    win = s[HALO:, :]
    if w > SUBLANES:
        assert w == 2 * SUBLANES
        win = win + s[HALO - SUBLANES:-SUBLANES, :]
    return win


def _ffn_chunk(h, f, c, wt):
    a = _wdot(h, wt.w_ff1[:, c:c + FF_CHUNK])
    a = jnp.square(jnp.maximum(a, 0.0)).astype(_BF16)
    part = _wdot(a, wt.w_ff2[c // 2:(c + FF_CHUNK) // 2, :])
    return part if f is None else f + part


def _front_stages(row0, seq_row0, layer, x_ref, sm, w_s_ref, b_s_ref, wt, zbuf_ref):
    tm = SUB_TILE
    d = x_ref.shape[1]
    rows = slice(row0, row0 + tm)
    zrows = slice(HALO + row0, HALO + row0 + tm)
    lyr = slice(layer, layer + 1)

    x = x_ref[rows, :]
    h = _rms(x, sm.pre_mix_g[lyr, :]).astype(_BF16)
    yield

    def proj(lo, hi):
        return _wdot(h, wt.w_in[:, lo:hi]) + sm.b_in[lyr, lo:hi]

    zbuf_ref[zrows, :] = proj(0, d)
    yield
    v = _gelu_tanh(proj(2 * d, 3 * d))
    yield

    t = seq_row0 + row0 + lax.broadcasted_iota(jnp.int32, (tm, LANES), 0)
    gdim = d // len(POOL_WINDOWS)
    pooled = []
    for k, w in enumerate(POOL_WINDOWS):
        zext = zbuf_ref[row0:row0 + HALO + tm, k * gdim:(k + 1) * gdim]
        win = _window_sums(zext, w)
        inv_cnt = 1.0 / jnp.minimum(t + 1, w).astype(_F32)
        inv_cnt = jnp.concatenate([inv_cnt] * (gdim // LANES), axis=1)
        pooled.append((win * inv_cnt - zext[HALO:, :]).astype(_BF16))
    yield
    gate_a = _sigmoid(proj(3 * d, 4 * d))
    yield
    pm = jnp.concatenate([_wdot(pooled[k], wt.pool_w[k]) for k in range(len(POOL_WINDOWS))], axis=-1)
    pm = (pm * sm.pool_scale[lyr, :]).astype(_BF16)
    merged = gate_a * _wdot(pm, wt.w_pa[...])
    yield

    mu = jnp.mean(v, axis=-1, keepdims=True)
    vc = v - mu
    vn = (vc * lax.rsqrt(jnp.mean(vc * vc, axis=-1, keepdims=True) + EPS) * sm.ln_g[lyr, :]
          + sm.ln_b[lyr, :]).astype(_BF16)
    yield
    u = _gelu_tanh(proj(d, 2 * d))
    yield
    hd = d // SGU_GROUPS
    n_chunks = tm // SGU_CHUNK
    causal = (lax.broadcasted_iota(jnp.int32, (SGU_CHUNK, SGU_CHUNK), 0)
              >= lax.broadcasted_iota(jnp.int32, (SGU_CHUNK, SGU_CHUNK), 1))
    blocks = [[] for _ in range(n_chunks)]
    for g in range(SGU_GROUPS):
        ws = jnp.where(causal, w_s_ref[g], 0.0).astype(_BF16)
        rhs = jnp.concatenate([vn[c * SGU_CHUNK:(c + 1) * SGU_CHUNK, g * hd:(g + 1) * hd]
                               for c in range(n_chunks)], axis=1)
        out = _dot(ws, rhs)
        for c in range(n_chunks):
            blocks[c].append(out[:, c * hd:(c + 1) * hd])
    spatial = jnp.concatenate([jnp.concatenate(b, axis=1) + b_s_ref[...] for b in blocks], axis=0)
    yield
    gate_b = _sigmoid(proj(4 * d, 5 * d))
    yield
    merged = merged + gate_b * _wdot((u * spatial).astype(_BF16), wt.w_pb[...])
    yield
    o = _wdot(merged.astype(_BF16), wt.w_o[...])
    yield
    x = x + _rms(o, sm.post_mix_g[lyr, :])
    h = _rms(x, sm.pre_ffn_g[lyr, :]).astype(_BF16)
    yield
    f = None
    for c in range(0, FRONT_FF_CHUNKS * FF_CHUNK, FF_CHUNK):
        f = _ffn_chunk(h, f, c, wt)
        yield
    return x, h, f


def _back_stages(x, h, f, p_bf16, layer, sm, wt):
    lyr = slice(layer, layer + 1)
    for c in range(FRONT_FF_CHUNKS * FF_CHUNK, wt.w_ff1.shape[1], FF_CHUNK):
        f = _ffn_chunk(h, f, c, wt)
        yield
    x = x + _rms(f, sm.post_ffn_g[lyr, :])
    yield
    gate = _sigmoid(_wdot(x.astype(_BF16), wt.w_gate[...]))
    e = _wdot(p_bf16, wt.w_proj[...])
    yield
    return x + _rms(gate * e, sm.post_ple_g[lyr, :])


def _layer_kernel(*refs, layer, tiles_per_seq, n_tiles):
    refs = list(refs)
    x_ref, p_ref, p_prev_ref = refs[:3]
    sm = SimpleNamespace(**dict(zip(_SMALL, refs[3:])))
    k = 3 + len(_SMALL)
    w_s_ref, b_s_ref = refs[k:k + 2]
    hbm = refs[k + 2:k + 2 + len(_BIG)]
    out_hbm = refs[k + 2 + len(_BIG)]
    packed = refs[k + 3 + len(_BIG):k + 3 + 2 * len(_BIG)]
    wt = SimpleNamespace(**dict(zip(_BIG, packed)))
    zbuf_ref, carry_x, carry_h, carry_f, stage_ref, osem_ref, wsem_ref = refs[k + 3 + 2 * len(_BIG):]

    tm, d = x_ref.shape
    assert stage_ref.shape == (STAGE_SLOTS, SUB_TILE, d) and STAGE_SLOTS == 4
    step = pl.program_id(0)
    j = step % tiles_per_seq
    par = step % 2

    def out_slot(parity, which):
        return 2 * parity + which

    def out_copy(parity, which, sub_tile):
        row = pl.multiple_of(sub_tile * SUB_TILE, SUB_TILE)
        slot = out_slot(parity, which)
        return pltpu.make_async_copy(stage_ref.at[slot], out_hbm.at[pl.ds(row, SUB_TILE), :],
                                     osem_ref.at[slot])

    @pl.when(step == 0)
    def _():
        pieces = []
        for name, src, dst in zip(_BIG, hbm, packed):
            if name == "pool_w":
                for g in range(len(POOL_WINDOWS)):
                    pieces += _matrix_pieces(src.at[layer, g], dst.at[g])
            else:
                pieces += _matrix_pieces(src.at[layer], dst)
        _load_weights(pieces, stage_ref, wsem_ref)
        carry_x[...] = jnp.zeros(carry_x.shape, carry_x.dtype)
        carry_h[...] = jnp.zeros(carry_h.shape, carry_h.dtype)
        carry_f[...] = jnp.zeros(carry_f.shape, carry_f.dtype)

    @pl.when(step >= 2)
    def _():
        out_copy(par, 1, 2 * (step - 2)).wait()

    @pl.when(step >= 3)
    def _():
        out_copy(par, 0, 2 * (step - 2) - 1).wait()

    def carried_program():
        state = (carry_x[...], carry_h[...], carry_f[...], p_prev_ref[...].astype(_BF16))
        stage_ref[out_slot(par, 0)] = yield from _back_stages(*state, layer, sm, wt)

    def lead_program():
        x, h, f = yield from _front_stages(0, j * tm, layer, x_ref, sm, w_s_ref, b_s_ref, wt, zbuf_ref)
        p_bf16 = p_ref[0:SUB_TILE, :].astype(_BF16)
        stage_ref[out_slot(par, 1)] = yield from _back_stages(x, h, f, p_bf16, layer, sm, wt)

    def trail_program():
        x, h, f = yield from _front_stages(SUB_TILE, j * tm, layer, x_ref, sm, w_s_ref, b_s_ref,
                                           wt, zbuf_ref)
        carry_x[...] = x
        carry_h[...] = h
        carry_f[...] = f

    @pl.when(step < n_tiles)
    def _():
        @pl.when(j == 0)
        def _():
            zbuf_ref[0:HALO, :] = jnp.zeros((HALO, d), _F32)

        @pl.when(j != 0)
        def _():
            zbuf_ref[0:HALO, :] = zbuf_ref[tm:tm + HALO, :]

        _emit_interleaved([(carried_program(), 0, 2), (lead_program(), 0, 1), (trail_program(), 4, 1)])
        out_copy(par, 1, 2 * step).start()

        @pl.when(step >= 1)
        def _():
            out_copy(par, 0, 2 * step - 1).start()

    @pl.when(step == n_tiles)
    def _():
        _emit_interleaved([(carried_program(), 0, 1)])
        out_copy(par, 0, 2 * step - 1).start()
        out_copy(par, 0, 2 * step - 1).wait()
        out_copy(1 - par, 1, 2 * (step - 1)).wait()
        out_copy(1 - par, 0, 2 * (step - 1) - 1).wait()


def _whole(a):
    nd = a.ndim
    return pl.BlockSpec(a.shape, lambda i: (0,) * nd, pipeline_mode=pl.Buffered(1))


def _of_layer(a, layer):
    nd = a.ndim
    return pl.BlockSpec((None,) + a.shape[1:], lambda i: (layer,) + (0,) * (nd - 1),
                        pipeline_mode=pl.Buffered(1))


def _layer_call(x, p, layer, seq, small, w_s, b_s_tile, big):
    n, d = x.shape
    tm = ROW_TILE
    n_tiles = n // tm
    assert n_tiles >= 2
    last = n_tiles - 1
    row = pl.BlockSpec((tm, d), lambda i: (jnp.minimum(i, last), 0))
    prow = pl.BlockSpec((None, tm, p.shape[2]), lambda i: (layer, jnp.minimum(i, last), 0))
    prow_prev = pl.BlockSpec((None, SUB_TILE, p.shape[2]),
                             lambda i: (layer, jnp.maximum(2 * i - 1, 0), 0))
    packed = [pltpu.VMEM(w.shape[1:-2] + (w.shape[-2] // 2, w.shape[-1]), _U32) for w in big]
    return pl.pallas_call(
        functools.partial(_layer_kernel, layer=layer, tiles_per_seq=seq // tm, n_tiles=n_tiles),
        out_shape=jax.ShapeDtypeStruct((n, d), _F32),
        grid=(n_tiles + 1,),
        in_specs=([row, prow, prow_prev] + [_whole(a) for a in small]
                  + [_of_layer(w_s, layer), _of_layer(b_s_tile, layer)]
                  + [pl.BlockSpec(memory_space=pl.ANY)] * len(big)),
        out_specs=pl.BlockSpec(memory_space=pl.ANY),
        scratch_shapes=packed + [
            pltpu.VMEM((HALO + tm, d), _F32),
            pltpu.VMEM((SUB_TILE, d), _F32),
            pltpu.VMEM((SUB_TILE, d), _BF16),
            pltpu.VMEM((SUB_TILE, d), _F32),
            pltpu.VMEM((STAGE_SLOTS, STAGE_ROWS, STAGE_COLS), _F32),
            pltpu.SemaphoreType.DMA((STAGE_SLOTS,)),
            pltpu.SemaphoreType.DMA((STAGE_SLOTS,))],
        compiler_params=pltpu.CompilerParams(
            dimension_semantics=("arbitrary",), vmem_limit_bytes=VMEM_LIMIT_BYTES),
        name="layer",
    )(x, p, p, *small, w_s, b_s_tile, *big)


def kernel(x, p, pre_mix_g, w_in, b_in, pool_w, pool_scale, sgu_ln_g, sgu_ln_b, sgu_w_s,
           sgu_b_s, w_pa, w_pb, w_o, post_mix_g, pre_ffn_g, w_ff1, w_ff2, post_ffn_g,
           w_ple_gate, w_ple_proj, post_ple_g):
    b, s, d = x.shape
    depth = w_in.shape[0]
    assert s % ROW_TILE == 0 and SUB_TILE % SGU_CHUNK == 0
    assert d % SGU_GROUPS == 0 and (d // len(POOL_WINDOWS)) % LANES == 0
    assert w_ff1.shape[2] % FF_CHUNK == 0
    xf = x.reshape(b * s, d)
    pf = p.reshape(depth, b * s, p.shape[-1])
    b_s_tile = jnp.repeat(jnp.swapaxes(sgu_b_s, 1, 2), d // SGU_GROUPS, axis=2)
    small = (pre_mix_g, b_in, pool_scale, sgu_ln_g, sgu_ln_b, post_mix_g,
             pre_ffn_g, post_ffn_g, post_ple_g)
    big = (w_in, pool_w, w_pa, w_pb, w_o, w_ff1, w_ff2, w_ple_gate, w_ple_proj)
    for i in range(depth):
        xf = _layer_call(xf, pf, i, s, small, sgu_w_s, b_s_tile, big)
    return xf.reshape(b, s, d)
```

```python
import functools
from types import SimpleNamespace

import jax
import jax.numpy as jnp
from jax import lax
from jax.experimental import pallas as pl
from jax.experimental.pallas import tpu as pltpu

POOL_WINDOWS = (2, 4, 8, 16)
SUBLANES, LANES = 8, 128
HALO = 16
SGU_CHUNK = 128
SGU_GROUPS = 8
EPS = 1e-6
FF_CHUNK = 1024
FRONT_FF_CHUNKS = 2
SUB_TILE = 256
ROW_TILE = 2 * SUB_TILE
STAGE_ROWS, STAGE_COLS = SUB_TILE, 1024
STAGE_SLOTS = 4
VMEM_LIMIT_BYTES = 60 * 1024 * 1024

_BF16 = jnp.bfloat16
_F32 = jnp.float32
_U32 = jnp.uint32

_SMALL = ("pre_mix_g", "b_in", "pool_scale", "ln_g", "ln_b", "post_mix_g",
          "pre_ffn_g", "post_ffn_g", "post_ple_g")
_BIG = ("w_in", "pool_w", "w_pa", "w_pb", "w_o", "w_ff1", "w_ff2", "w_gate", "w_proj")
_PACKED = tuple(n for n in _BIG if n != "pool_w")


def _rms(x, g):
    return x * lax.rsqrt(jnp.mean(x * x, axis=-1, keepdims=True) + EPS) * g


def _dot(a, b):
    return jnp.dot(a, b, preferred_element_type=_F32)


def _wdot(a, w_packed):
    return _dot(a, pltpu.bitcast(w_packed, _BF16))


_LOG2E = 1.4426950408889634
_GELU_C1 = -2.0 * 0.7978845608028654 * _LOG2E
_GELU_C2 = _GELU_C1 * 0.044715


def _sigmoid(x):
    return 1.0 / (1.0 + jnp.exp2(x * -_LOG2E))


def _gelu_tanh(x):
    return x / (1.0 + jnp.exp2(x * (_GELU_C1 + _GELU_C2 * (x * x))))


def _matrix_pieces(src, dst):
    k, n = src.shape
    pieces = []
    for r in range(0, k, STAGE_ROWS):
        nr = min(STAGE_ROWS, k - r)
        for c in range(0, n, STAGE_COLS):
            nc = min(STAGE_COLS, n - c)
            pieces.append((src.at[pl.ds(r, nr), pl.ds(c, nc)],
                           dst.at[pl.ds(r // 2, nr // 2), pl.ds(c, nc)]))
    return pieces


def _load_weights(pieces, stage_ref, sem_ref):
    slots = stage_ref.shape[0]

    def copy(i):
        src = pieces[i][0]
        nr, nc = src.shape
        return pltpu.make_async_copy(
            src, stage_ref.at[i % slots, pl.ds(0, nr), pl.ds(0, nc)], sem_ref.at[i % slots])

    for i in range(min(slots, len(pieces))):
        copy(i).start()
    for i, (src, dst) in enumerate(pieces):
        nr, nc = src.shape
        copy(i).wait()
        dst[...] = pltpu.bitcast(stage_ref[i % slots, 0:nr, 0:nc].astype(_BF16), _U32)
        if i + slots < len(pieces):
            copy(i + slots).start()


def _fold_pool_into_pa(pool_hbm, scale, w_pa_hbm, dst, stage_ref, sem_ref):
    n_groups, gdim, _ = pool_hbm.shape
    assert gdim == STAGE_ROWS and stage_ref.shape[0] >= 4

    def copies(k):
        slot = 2 * (k % 2)
        return (pltpu.make_async_copy(pool_hbm.at[k],
                                      stage_ref.at[slot, pl.ds(0, gdim), pl.ds(0, gdim)],
                                      sem_ref.at[slot]),
                pltpu.make_async_copy(w_pa_hbm.at[pl.ds(k * gdim, gdim), :], stage_ref.at[slot + 1],
                                      sem_ref.at[slot + 1]))

    for k in range(min(2, n_groups)):
        for c in copies(k):
            c.start()
    for k in range(n_groups):
        slot = 2 * (k % 2)
        for c in copies(k):
            c.wait()
        pw = (stage_ref[slot, 0:gdim, 0:gdim] * scale[:, k * gdim:(k + 1) * gdim]).astype(_BF16)
        folded = _dot(pw, stage_ref[slot + 1].astype(_BF16))
        dst[k * gdim // 2:(k + 1) * gdim // 2, :] = pltpu.bitcast(folded.astype(_BF16), _U32)
        if k + 2 < n_groups:
            for c in copies(k + 2):
                c.start()


def _emit_interleaved(schedule):
    live = list(schedule)
    tick = 0
    while live:
        for item in list(live):
            prog, first, period = item
            if tick >= first and (tick - first) % period == 0:
                if next(prog, StopIteration) is StopIteration:
                    live.remove(item)
        tick += 1


def _window_sums(zext, w):
    s = zext
    k = 1
    while k < min(w, SUBLANES):
        s = s + pltpu.roll(s, k, axis=0)
        k *= 2
    win = s[HALO:, :]
    if w > SUBLANES:
        assert w == 2 * SUBLANES
        win = win + s[HALO - SUBLANES:-SUBLANES, :]
    return win


def _ffn_chunk(h, f, c, wt):
    a = _wdot(h, wt.w_ff1[:, c:c + FF_CHUNK])
    a = jnp.square(jnp.maximum(a, 0.0)).astype(_BF16)
    part = _wdot(a, wt.w_ff2[c // 2:(c + FF_CHUNK) // 2, :])
    return part if f is None else f + part


def _front_stages(row0, seq_row0, layer, x_ref, sm, w_s_ref, b_s_ref, wt, zbuf_ref):
    tm = SUB_TILE
    d = x_ref.shape[1]
    rows = slice(row0, row0 + tm)
    zrows = slice(HALO + row0, HALO + row0 + tm)
    lyr = slice(layer, layer + 1)

    x = x_ref[rows, :]
    h = _rms(x, sm.pre_mix_g[lyr, :]).astype(_BF16)
    yield

    def proj(lo, hi):
        return _wdot(h, wt.w_in[:, lo:hi]) + sm.b_in[lyr, lo:hi]

    zbuf_ref[zrows, :] = proj(0, d)
    yield
    v = _gelu_tanh(proj(2 * d, 3 * d))
    yield

    t = seq_row0 + row0 + lax.broadcasted_iota(jnp.int32, (tm, LANES), 0)
    gdim = d // len(POOL_WINDOWS)
    pooled = []
    for k, w in enumerate(POOL_WINDOWS):
        zext = zbuf_ref[row0:row0 + HALO + tm, k * gdim:(k + 1) * gdim]
        win = _window_sums(zext, w)
        inv_cnt = 1.0 / jnp.minimum(t + 1, w).astype(_F32)
        inv_cnt = jnp.concatenate([inv_cnt] * (gdim // LANES), axis=1)
        pooled.append((win * inv_cnt - zext[HALO:, :]).astype(_BF16))
    yield
    gate_a = _sigmoid(proj(3 * d, 4 * d))
    yield
    merged = gate_a * _wdot(jnp.concatenate(pooled, axis=-1), wt.w_pa[...])
    yield

    mu = jnp.mean(v, axis=-1, keepdims=True)
    vc = v - mu
    vn = (vc * lax.rsqrt(jnp.mean(vc * vc, axis=-1, keepdims=True) + EPS) * sm.ln_g[lyr, :]
          + sm.ln_b[lyr, :]).astype(_BF16)
    yield
    u = _gelu_tanh(proj(d, 2 * d))
    yield
    hd = d // SGU_GROUPS
    n_chunks = tm // SGU_CHUNK
    causal = (lax.broadcasted_iota(jnp.int32, (SGU_CHUNK, SGU_CHUNK), 0)
              >= lax.broadcasted_iota(jnp.int32, (SGU_CHUNK, SGU_CHUNK), 1))
    blocks = [[] for _ in range(n_chunks)]
    for g in range(SGU_GROUPS):
        ws = jnp.where(causal, w_s_ref[g], 0.0).astype(_BF16)
        rhs = jnp.concatenate([vn[c * SGU_CHUNK:(c + 1) * SGU_CHUNK, g * hd:(g + 1) * hd]
                               for c in range(n_chunks)], axis=1)
        out = _dot(ws, rhs)
        for c in range(n_chunks):
            blocks[c].append(out[:, c * hd:(c + 1) * hd])
    spatial = jnp.concatenate([jnp.concatenate(b, axis=1) + b_s_ref[...] for b in blocks], axis=0)
    yield
    gate_b = _sigmoid(proj(4 * d, 5 * d))
    yield
    merged = merged + gate_b * _wdot((u * spatial).astype(_BF16), wt.w_pb[...])
    yield
    o = _wdot(merged.astype(_BF16), wt.w_o[...])
    yield
    x = x + _rms(o, sm.post_mix_g[lyr, :])
    h = _rms(x, sm.pre_ffn_g[lyr, :]).astype(_BF16)
    yield
    f = None
    for c in range(0, FRONT_FF_CHUNKS * FF_CHUNK, FF_CHUNK):
        f = _ffn_chunk(h, f, c, wt)
        yield
    return x, h, f


def _back_stages(x, h, f, p_bf16, layer, sm, wt):
    lyr = slice(layer, layer + 1)
    for c in range(FRONT_FF_CHUNKS * FF_CHUNK, wt.w_ff1.shape[1], FF_CHUNK):
        f = _ffn_chunk(h, f, c, wt)
        yield
    x = x + _rms(f, sm.post_ffn_g[lyr, :])
    yield
    gate = _sigmoid(_wdot(x.astype(_BF16), wt.w_gate[...]))
    e = _wdot(p_bf16, wt.w_proj[...])
    yield
    return x + _rms(gate * e, sm.post_ple_g[lyr, :])


def _layer_kernel(*refs, layer, tiles_per_seq, n_tiles):
    refs = list(refs)
    x_ref, p_ref, p_prev_ref = refs[:3]
    sm = SimpleNamespace(**dict(zip(_SMALL, refs[3:])))
    k = 3 + len(_SMALL)
    w_s_ref, b_s_ref = refs[k:k + 2]
    hbm = refs[k + 2:k + 2 + len(_BIG)]
    out_hbm = refs[k + 2 + len(_BIG)]
    n_packed = len(_PACKED)
    packed = dict(zip(_PACKED, refs[k + 3 + len(_BIG):k + 3 + len(_BIG) + n_packed]))
    wt = SimpleNamespace(**packed)
    (zbuf_ref, carry_x, carry_h, carry_f, stage_ref, osem_ref,
     wsem_ref) = refs[k + 3 + len(_BIG) + n_packed:]

    tm, d = x_ref.shape
    assert stage_ref.shape == (STAGE_SLOTS, SUB_TILE, d) and STAGE_SLOTS == 4
    step = pl.program_id(0)
    j = step % tiles_per_seq
    par = step % 2

    def out_slot(parity, which):
        return 2 * parity + which

    def out_copy(parity, which, sub_tile):
        row = pl.multiple_of(sub_tile * SUB_TILE, SUB_TILE)
        slot = out_slot(parity, which)
        return pltpu.make_async_copy(stage_ref.at[slot], out_hbm.at[pl.ds(row, SUB_TILE), :],
                                     osem_ref.at[slot])

    @pl.when(step == 0)
    def _():
        src = {name: ref.at[layer] for name, ref in zip(_BIG, hbm)}
        pieces = []
        for name in _PACKED:
            if name != "w_pa":
                pieces += _matrix_pieces(src[name], packed[name])
        _load_weights(pieces, stage_ref, wsem_ref)
        _fold_pool_into_pa(src["pool_w"], sm.pool_scale[layer:layer + 1, :], src["w_pa"],
                           packed["w_pa"], stage_ref, wsem_ref)
        carry_x[...] = jnp.zeros(carry_x.shape, carry_x.dtype)
        carry_h[...] = jnp.zeros(carry_h.shape, carry_h.dtype)
        carry_f[...] = jnp.zeros(carry_f.shape, carry_f.dtype)

    @pl.when(step >= 2)
    def _():
        out_copy(par, 1, 2 * (step - 2)).wait()

    @pl.when(step >= 3)
    def _():
        out_copy(par, 0, 2 * (step - 2) - 1).wait()

    def carried_program():
        state = (carry_x[...], carry_h[...], carry_f[...], p_prev_ref[...].astype(_BF16))
        stage_ref[out_slot(par, 0)] = yield from _back_stages(*state, layer, sm, wt)

    def lead_program():
        x, h, f = yield from _front_stages(0, j * tm, layer, x_ref, sm, w_s_ref, b_s_ref, wt, zbuf_ref)
        p_bf16 = p_ref[0:SUB_TILE, :].astype(_BF16)
        stage_ref[out_slot(par, 1)] = yield from _back_stages(x, h, f, p_bf16, layer, sm, wt)

    def trail_program():
        x, h, f = yield from _front_stages(SUB_TILE, j * tm, layer, x_ref, sm, w_s_ref, b_s_ref,
                                           wt, zbuf_ref)
        carry_x[...] = x
        carry_h[...] = h
        carry_f[...] = f

    @pl.when(step < n_tiles)
    def _():
        @pl.when(j == 0)
        def _():
            zbuf_ref[0:HALO, :] = jnp.zeros((HALO, d), _F32)

        @pl.when(j != 0)
        def _():
            zbuf_ref[0:HALO, :] = zbuf_ref[tm:tm + HALO, :]

        _emit_interleaved([(carried_program(), 0, 2), (lead_program(), 0, 1), (trail_program(), 4, 1)])
        out_copy(par, 1, 2 * step).start()

        @pl.when(step >= 1)
        def _():
            out_copy(par, 0, 2 * step - 1).start()

    @pl.when(step == n_tiles)
    def _():
        _emit_interleaved([(carried_program(), 0, 1)])
        out_copy(par, 0, 2 * step - 1).start()
        out_copy(par, 0, 2 * step - 1).wait()
        out_copy(1 - par, 1, 2 * (step - 1)).wait()
        out_copy(1 - par, 0, 2 * (step - 1) - 1).wait()


def _whole(a):
    nd = a.ndim
    return pl.BlockSpec(a.shape, lambda i: (0,) * nd, pipeline_mode=pl.Buffered(1))


def _of_layer(a, layer):
    nd = a.ndim
    return pl.BlockSpec((None,) + a.shape[1:], lambda i: (layer,) + (0,) * (nd - 1),
                        pipeline_mode=pl.Buffered(1))


def _layer_call(x, p, layer, seq, small, w_s, b_s_tile, big):
    n, d = x.shape
    tm = ROW_TILE
    n_tiles = n // tm
    assert n_tiles >= 2
    last = n_tiles - 1
    row = pl.BlockSpec((tm, d), lambda i: (jnp.minimum(i, last), 0))
    prow = pl.BlockSpec((None, tm, p.shape[2]), lambda i: (layer, jnp.minimum(i, last), 0))
    prow_prev = pl.BlockSpec((None, SUB_TILE, p.shape[2]),
                             lambda i: (layer, jnp.maximum(2 * i - 1, 0), 0))
    packed = [pltpu.VMEM((w.shape[-2] // 2, w.shape[-1]), _U32)
              for name, w in zip(_BIG, big) if name in _PACKED]
    return pl.pallas_call(
        functools.partial(_layer_kernel, layer=layer, tiles_per_seq=seq // tm, n_tiles=n_tiles),
        out_shape=jax.ShapeDtypeStruct((n, d), _F32),
        grid=(n_tiles + 1,),
        in_specs=([row, prow, prow_prev] + [_whole(a) for a in small]
                  + [_of_layer(w_s, layer), _of_layer(b_s_tile, layer)]
                  + [pl.BlockSpec(memory_space=pl.ANY)] * len(big)),
        out_specs=pl.BlockSpec(memory_space=pl.ANY),
        scratch_shapes=packed + [
            pltpu.VMEM((HALO + tm, d), _F32),
            pltpu.VMEM((SUB_TILE, d), _F32),
            pltpu.VMEM((SUB_TILE, d), _BF16),
            pltpu.VMEM((SUB_TILE, d), _F32),
            pltpu.VMEM((STAGE_SLOTS, STAGE_ROWS, STAGE_COLS), _F32),
            pltpu.SemaphoreType.DMA((STAGE_SLOTS,)),
            pltpu.SemaphoreType.DMA((STAGE_SLOTS,))],
        compiler_params=pltpu.CompilerParams(
            dimension_semantics=("arbitrary",), vmem_limit_bytes=VMEM_LIMIT_BYTES),
        name="layer",
    )(x, p, p, *small, w_s, b_s_tile, *big)


def kernel(x, p, pre_mix_g, w_in, b_in, pool_w, pool_scale, sgu_ln_g, sgu_ln_b, sgu_w_s,
           sgu_b_s, w_pa, w_pb, w_o, post_mix_g, pre_ffn_g, w_ff1, w_ff2, post_ffn_g,
           w_ple_gate, w_ple_proj, post_ple_g):
    b, s, d = x.shape
    depth = w_in.shape[0]
    assert s % ROW_TILE == 0 and SUB_TILE % SGU_CHUNK == 0
    assert d % SGU_GROUPS == 0 and (d // len(POOL_WINDOWS)) % LANES == 0
    assert w_ff1.shape[2] % FF_CHUNK == 0
    xf = x.reshape(b * s, d)
    pf = p.reshape(depth, b * s, p.shape[-1])
    b_s_tile = jnp.repeat(jnp.swapaxes(sgu_b_s, 1, 2), d // SGU_GROUPS, axis=2)
    small = (pre_mix_g, b_in, pool_scale, sgu_ln_g, sgu_ln_b, post_mix_g,
             pre_ffn_g, post_ffn_g, post_ple_g)
    big = (w_in, pool_w, w_pa, w_pb, w_o, w_ff1, w_ff2, w_ple_gate, w_ple_proj)
    for i in range(depth):
        xf = _layer_call(xf, pf, i, s, small, sgu_w_s, b_s_tile, big)
    return xf.reshape(b, s, d)
```

```python
import functools
from types import SimpleNamespace

import jax
import jax.numpy as jnp
from jax import lax
from jax.experimental import pallas as pl
from jax.experimental.pallas import tpu as pltpu

POOL_WINDOWS = (2, 4, 8, 16)
SUBLANES, LANES = 8, 128
HALO = 16
SGU_CHUNK = 128
SGU_GROUPS = 8
EPS = 1e-6
FF_CHUNK = 1024
FRONT_FF_CHUNKS = 2
SUB_TILE = 256
ROW_TILE = 2 * SUB_TILE
STAGE_ROWS, STAGE_COLS = SUB_TILE, 1024
STAGE_SLOTS = 4
VMEM_LIMIT_BYTES = 60 * 1024 * 1024

_BF16 = jnp.bfloat16
_F32 = jnp.float32
_U32 = jnp.uint32

_SMALL = ("pre_mix_g", "b_in", "pool_scale", "ln_g", "ln_b", "post_mix_g",
          "pre_ffn_g", "post_ffn_g", "post_ple_g")
_BIG = ("w_in", "pool_w", "w_pa", "w_pb", "w_o", "w_ff1", "w_ff2", "w_gate", "w_proj")


def _rms(x, g):
    return x * lax.rsqrt(jnp.mean(x * x, axis=-1, keepdims=True) + EPS) * g


def _dot(a, b):
    return jnp.dot(a, b, preferred_element_type=_F32)


def _wdot(a, w_packed):
    return _dot(a, pltpu.bitcast(w_packed, _BF16))


_GELU_K1 = 0.7978845608028654
_GELU_K2 = _GELU_K1 * 0.044715


def _sigmoid(x):
    return 0.5 * jnp.tanh(0.5 * x) + 0.5


def _gelu_tanh(x):
    half_x = 0.5 * x
    return half_x * jnp.tanh(x * (_GELU_K1 + _GELU_K2 * (x * x))) + half_x


def _matrix_pieces(src, dst):
    k, n = src.shape
    pieces = []
    for r in range(0, k, STAGE_ROWS):
        nr = min(STAGE_ROWS, k - r)
        for c in range(0, n, STAGE_COLS):
            nc = min(STAGE_COLS, n - c)
            pieces.append((src.at[pl.ds(r, nr), pl.ds(c, nc)],
                           dst.at[pl.ds(r // 2, nr // 2), pl.ds(c, nc)]))
    return pieces


def _load_weights(pieces, stage_ref, sem_ref):
    slots = stage_ref.shape[0]

    def copy(i):
        src = pieces[i][0]
        nr, nc = src.shape
        return pltpu.make_async_copy(
            src, stage_ref.at[i % slots, pl.ds(0, nr), pl.ds(0, nc)], sem_ref.at[i % slots])

    for i in range(min(slots, len(pieces))):
        copy(i).start()
    for i, (src, dst) in enumerate(pieces):
        nr, nc = src.shape
        copy(i).wait()
        dst[...] = pltpu.bitcast(stage_ref[i % slots, 0:nr, 0:nc].astype(_BF16), _U32)
        if i + slots < len(pieces):
            copy(i + slots).start()


def _emit_interleaved(schedule):
    live = list(schedule)
    tick = 0
    while live:
        for item in list(live):
            prog, first, period = item
            if tick >= first and (tick - first) % period == 0:
                if next(prog, StopIteration) is StopIteration:
                    live.remove(item)
        tick += 1


def _window_sums(zext, w):
    s = zext
    k = 1
    while k < min(w, SUBLANES):
        s = s + pltpu.roll(s, k, axis=0)
        k *= 2
    win = s[HALO:, :]
    if w > SUBLANES:
        assert w == 2 * SUBLANES
        win = win + s[HALO - SUBLANES:-SUBLANES, :]
    return win


def _ffn_chunk(h, f, c, wt):
    a = _wdot(h, wt.w_ff1[:, c:c + FF_CHUNK])
    a = jnp.square(jnp.maximum(a, 0.0)).astype(_BF16)
    part = _wdot(a, wt.w_ff2[c // 2:(c + FF_CHUNK) // 2, :])
    return part if f is None else f + part


def _front_stages(row0, seq_row0, layer, x_ref, sm, w_s_ref, b_s_ref, wt, zbuf_ref):
    tm = SUB_TILE
    d = x_ref.shape[1]
    rows = slice(row0, row0 + tm)
    zrows = slice(HALO + row0, HALO + row0 + tm)
    lyr = slice(layer, layer + 1)

    x = x_ref[rows, :]
    h = _rms(x, sm.pre_mix_g[lyr, :]).astype(_BF16)
    yield

    def proj(lo, hi):
        return _wdot(h, wt.w_in[:, lo:hi]) + sm.b_in[lyr, lo:hi]

    zbuf_ref[zrows, :] = proj(0, d)
    yield
    v = _gelu_tanh(proj(2 * d, 3 * d))
    yield

    t = seq_row0 + row0 + lax.broadcasted_iota(jnp.int32, (tm, LANES), 0)
    gdim = d // len(POOL_WINDOWS)
    pooled = []
    for k, w in enumerate(POOL_WINDOWS):
        zext = zbuf_ref[row0:row0 + HALO + tm, k * gdim:(k + 1) * gdim]
        win = _window_sums(zext, w)
        inv_cnt = 1.0 / jnp.minimum(t + 1, w).astype(_F32)
        inv_cnt = jnp.concatenate([inv_cnt] * (gdim // LANES), axis=1)
        pooled.append((win * inv_cnt - zext[HALO:, :]).astype(_BF16))
    yield
    gate_a = _sigmoid(proj(3 * d, 4 * d))
    yield
    pm = jnp.concatenate([_wdot(pooled[k], wt.pool_w[k]) for k in range(len(POOL_WINDOWS))], axis=-1)
    pm = (pm * sm.pool_scale[lyr, :]).astype(_BF16)
    merged = gate_a * _wdot(pm, wt.w_pa[...])
    yield

    mu = jnp.mean(v, axis=-1, keepdims=True)
    vc = v - mu
    vn = (vc * lax.rsqrt(jnp.mean(vc * vc, axis=-1, keepdims=True) + EPS) * sm.ln_g[lyr, :]
          + sm.ln_b[lyr, :]).astype(_BF16)
    yield
    u = _gelu_tanh(proj(d, 2 * d))
    yield
    hd = d // SGU_GROUPS
    n_chunks = tm // SGU_CHUNK
    causal = (lax.broadcasted_iota(jnp.int32, (SGU_CHUNK, SGU_CHUNK), 0)
              >= lax.broadcasted_iota(jnp.int32, (SGU_CHUNK, SGU_CHUNK), 1))
    blocks = [[] for _ in range(n_chunks)]
    for g in range(SGU_GROUPS):
        ws = jnp.where(causal, w_s_ref[g], 0.0).astype(_BF16)
        rhs = jnp.concatenate([vn[c * SGU_CHUNK:(c + 1) * SGU_CHUNK, g * hd:(g + 1) * hd]
                               for c in range(n_chunks)], axis=1)
        out = _dot(ws, rhs)
        for c in range(n_chunks):
            blocks[c].append(out[:, c * hd:(c + 1) * hd])
    spatial = jnp.concatenate([jnp.concatenate(b, axis=1) + b_s_ref[...] for b in blocks], axis=0)
    yield
    gate_b = _sigmoid(proj(4 * d, 5 * d))
    yield
    merged = merged + gate_b * _wdot((u * spatial).astype(_BF16), wt.w_pb[...])
    yield
    o = _wdot(merged.astype(_BF16), wt.w_o[...])
    yield
    x = x + _rms(o, sm.post_mix_g[lyr, :])
    h = _rms(x, sm.pre_ffn_g[lyr, :]).astype(_BF16)
    yield
    f = None
    for c in range(0, FRONT_FF_CHUNKS * FF_CHUNK, FF_CHUNK):
        f = _ffn_chunk(h, f, c, wt)
        yield
    return x, h, f


def _back_stages(x, h, f, p_bf16, layer, sm, wt):
    lyr = slice(layer, layer + 1)
    for c in range(FRONT_FF_CHUNKS * FF_CHUNK, wt.w_ff1.shape[1], FF_CHUNK):
        f = _ffn_chunk(h, f, c, wt)
        yield
    x = x + _rms(f, sm.post_ffn_g[lyr, :])
    yield
    gate = _sigmoid(_wdot(x.astype(_BF16), wt.w_gate[...]))
    e = _wdot(p_bf16, wt.w_proj[...])
    yield
    return x + _rms(gate * e, sm.post_ple_g[lyr, :])


def _layer_kernel(*refs, layer, tiles_per_seq, n_tiles):
    refs = list(refs)
    x_ref, p_ref, p_prev_ref = refs[:3]
    sm = SimpleNamespace(**dict(zip(_SMALL, refs[3:])))
    k = 3 + len(_SMALL)
    w_s_ref, b_s_ref = refs[k:k + 2]
    hbm = refs[k + 2:k + 2 + len(_BIG)]
    out_hbm = refs[k + 2 + len(_BIG)]
    packed = refs[k + 3 + len(_BIG):k + 3 + 2 * len(_BIG)]
    wt = SimpleNamespace(**dict(zip(_BIG, packed)))
    zbuf_ref, carry_x, carry_h, carry_f, stage_ref, osem_ref, wsem_ref = refs[k + 3 + 2 * len(_BIG):]

    tm, d = x_ref.shape
    assert stage_ref.shape == (STAGE_SLOTS, SUB_TILE, d) and STAGE_SLOTS == 4
    step = pl.program_id(0)
    j = step % tiles_per_seq
    par = step % 2

    def out_slot(parity, which):
        return 2 * parity + which

    def out_copy(parity, which, sub_tile):
        row = pl.multiple_of(sub_tile * SUB_TILE, SUB_TILE)
        slot = out_slot(parity, which)
        return pltpu.make_async_copy(stage_ref.at[slot], out_hbm.at[pl.ds(row, SUB_TILE), :],
                                     osem_ref.at[slot])

    @pl.when(step == 0)
    def _():
        pieces = []
        for name, src, dst in zip(_BIG, hbm, packed):
            if name == "pool_w":
                for g in range(len(POOL_WINDOWS)):
                    pieces += _matrix_pieces(src.at[layer, g], dst.at[g])
            else:
                pieces += _matrix_pieces(src.at[layer], dst)
        _load_weights(pieces, stage_ref, wsem_ref)
        carry_x[...] = jnp.zeros(carry_x.shape, carry_x.dtype)
        carry_h[...] = jnp.zeros(carry_h.shape, carry_h.dtype)
        carry_f[...] = jnp.zeros(carry_f.shape, carry_f.dtype)

    @pl.when(step >= 2)
    def _():
        out_copy(par, 1, 2 * (step - 2)).wait()

    @pl.when(step >= 3)
    def _():
        out_copy(par, 0, 2 * (step - 2) - 1).wait()

    def carried_program():
        state = (carry_x[...], carry_h[...], carry_f[...], p_prev_ref[...].astype(_BF16))
        stage_ref[out_slot(par, 0)] = yield from _back_stages(*state, layer, sm, wt)

    def lead_program():
        x, h, f = yield from _front_stages(0, j * tm, layer, x_ref, sm, w_s_ref, b_s_ref, wt, zbuf_ref)
        p_bf16 = p_ref[0:SUB_TILE, :].astype(_BF16)
        stage_ref[out_slot(par, 1)] = yield from _back_stages(x, h, f, p_bf16, layer, sm, wt)

    def trail_program():
        x, h, f = yield from _front_stages(SUB_TILE, j * tm, layer, x_ref, sm, w_s_ref, b_s_ref,
                                           wt, zbuf_ref)
        carry_x[...] = x
        carry_h[...] = h
        carry_f[...] = f

    @pl.when(step < n_tiles)
    def _():
        @pl.when(j == 0)
        def _():
            zbuf_ref[0:HALO, :] = jnp.zeros((HALO, d), _F32)

        @pl.when(j != 0)
        def _():
            zbuf_ref[0:HALO, :] = zbuf_ref[tm:tm + HALO, :]

        _emit_interleaved([(carried_program(), 0, 2), (lead_program(), 0, 1), (trail_program(), 4, 1)])
        out_copy(par, 1, 2 * step).start()

        @pl.when(step >= 1)
        def _():
            out_copy(par, 0, 2 * step - 1).start()

    @pl.when(step == n_tiles)
    def _():
        _emit_interleaved([(carried_program(), 0, 1)])
        out_copy(par, 0, 2 * step - 1).start()
        out_copy(par, 0, 2 * step - 1).wait()
        out_copy(1 - par, 1, 2 * (step - 1)).wait()
        out_copy(1 - par, 0, 2 * (step - 1) - 1).wait()


def _whole(a):
    nd = a.ndim
    return pl.BlockSpec(a.shape, lambda i: (0,) * nd, pipeline_mode=pl.Buffered(1))


def _of_layer(a, layer):
    nd = a.ndim
    return pl.BlockSpec((None,) + a.shape[1:], lambda i: (layer,) + (0,) * (nd - 1),
                        pipeline_mode=pl.Buffered(1))


def _layer_call(x, p, layer, seq, small, w_s, b_s_tile, big):
    n, d = x.shape
    tm = ROW_TILE
    n_tiles = n // tm
    assert n_tiles >= 2
    last = n_tiles - 1
    row = pl.BlockSpec((tm, d), lambda i: (jnp.minimum(i, last), 0))
    prow = pl.BlockSpec((None, tm, p.shape[2]), lambda i: (layer, jnp.minimum(i, last), 0))
    prow_prev = pl.BlockSpec((None, SUB_TILE, p.shape[2]),
                             lambda i: (layer, jnp.maximum(2 * i - 1, 0), 0))
    packed = [pltpu.VMEM(w.shape[1:-2] + (w.shape[-2] // 2, w.shape[-1]), _U32) for w in big]
    return pl.pallas_call(
        functools.partial(_layer_kernel, layer=layer, tiles_per_seq=seq // tm, n_tiles=n_tiles),
        out_shape=jax.ShapeDtypeStruct((n, d), _F32),
        grid=(n_tiles + 1,),
        in_specs=([row, prow, prow_prev] + [_whole(a) for a in small]
                  + [_of_layer(w_s, layer), _of_layer(b_s_tile, layer)]
                  + [pl.BlockSpec(memory_space=pl.ANY)] * len(big)),
        out_specs=pl.BlockSpec(memory_space=pl.ANY),
        scratch_shapes=packed + [
            pltpu.VMEM((HALO + tm, d), _F32),
            pltpu.VMEM((SUB_TILE, d), _F32),
            pltpu.VMEM((SUB_TILE, d), _BF16),
            pltpu.VMEM((SUB_TILE, d), _F32),
            pltpu.VMEM((STAGE_SLOTS, STAGE_ROWS, STAGE_COLS), _F32),
            pltpu.SemaphoreType.DMA((STAGE_SLOTS,)),
            pltpu.SemaphoreType.DMA((STAGE_SLOTS,))],
        compiler_params=pltpu.CompilerParams(
            dimension_semantics=("arbitrary",), vmem_limit_bytes=VMEM_LIMIT_BYTES),
        name="layer",
    )(x, p, p, *small, w_s, b_s_tile, *big)


def kernel(x, p, pre_mix_g, w_in, b_in, pool_w, pool_scale, sgu_ln_g, sgu_ln_b, sgu_w_s,
           sgu_b_s, w_pa, w_pb, w_o, post_mix_g, pre_ffn_g, w_ff1, w_ff2, post_ffn_g,
           w_ple_gate, w_ple_proj, post_ple_g):
    b, s, d = x.shape
    depth = w_in.shape[0]
    assert s % ROW_TILE == 0 and SUB_TILE % SGU_CHUNK == 0
    assert d % SGU_GROUPS == 0 and (d // len(POOL_WINDOWS)) % LANES == 0
    assert w_ff1.shape[2] % FF_CHUNK == 0
    xf = x.reshape(b * s, d)
    pf = p.reshape(depth, b * s, p.shape[-1])
    b_s_tile = jnp.repeat(jnp.swapaxes(sgu_b_s, 1, 2), d // SGU_GROUPS, axis=2)
    small = (pre_mix_g, b_in, pool_scale, sgu_ln_g, sgu_ln_b, post_mix_g,
             pre_ffn_g, post_ffn_g, post_ple_g)
    big = (w_in, pool_w, w_pa, w_pb, w_o, w_ff1, w_ff2, w_ple_gate, w_ple_proj)
    for i in range(depth):
        xf = _layer_call(xf, pf, i, s, small, sgu_w_s, b_s_tile, big)
    return xf.reshape(b, s, d)
```

```python
import functools
from types import SimpleNamespace

import jax
import jax.numpy as jnp
from jax import lax
from jax.experimental import pallas as pl
from jax.experimental.pallas import tpu as pltpu

POOL_WINDOWS = (2, 4, 8, 16)
SUBLANES, LANES = 8, 128
HALO = 16
SGU_CHUNK = 128
SGU_GROUPS = 8
EPS = 1e-6
FF_CHUNK = 1024
FRONT_FF_CHUNKS = 2
SUB_TILE = 256
ROW_TILE = 2 * SUB_TILE
STAGE_ROWS, STAGE_COLS = SUB_TILE, 1024
STAGE_SLOTS = 4
VMEM_LIMIT_BYTES = 60 * 1024 * 1024

_BF16 = jnp.bfloat16
_F32 = jnp.float32
_U32 = jnp.uint32

_SMALL = ("pre_mix_g", "b_in", "pool_scale", "ln_g", "ln_b", "post_mix_g",
          "pre_ffn_g", "post_ffn_g", "post_ple_g")
_BIG = ("w_in", "pool_w", "w_pa", "w_pb", "w_o", "w_ff1", "w_ff2", "w_gate", "w_proj")


def _rms(x, g):
    return x * lax.rsqrt(jnp.mean(x * x, axis=-1, keepdims=True) + EPS) * g


def _dot(a, b):
    return jnp.dot(a, b, preferred_element_type=_F32)


def _wdot(a, w_packed):
    return _dot(a, pltpu.bitcast(w_packed, _BF16))


ACT_WEIGHT_SCALE = 0.5
_GELU_K1 = 2.0 * 0.7978845608028654
_GELU_K2 = 8.0 * 0.7978845608028654 * 0.044715


def _sigmoid_of_half(half_x):
    return 0.5 * jnp.tanh(half_x) + 0.5


def _gelu_tanh_of_half(half_x):
    return half_x * jnp.tanh(half_x * (_GELU_K1 + _GELU_K2 * (half_x * half_x))) + half_x


def _matrix_pieces(src, dst, halved_from_col=None):
    k, n = src.shape
    pieces = []
    for r in range(0, k, STAGE_ROWS):
        nr = min(STAGE_ROWS, k - r)
        for c in range(0, n, STAGE_COLS):
            nc = min(STAGE_COLS, n - c)
            halved = halved_from_col is not None and c >= halved_from_col
            assert halved_from_col is None or halved or c + nc <= halved_from_col
            pieces.append((src.at[pl.ds(r, nr), pl.ds(c, nc)],
                           dst.at[pl.ds(r // 2, nr // 2), pl.ds(c, nc)],
                           ACT_WEIGHT_SCALE if halved else None))
    return pieces


def _load_weights(pieces, stage_ref, sem_ref):
    slots = stage_ref.shape[0]

    def copy(i):
        src = pieces[i][0]
        nr, nc = src.shape
        return pltpu.make_async_copy(
            src, stage_ref.at[i % slots, pl.ds(0, nr), pl.ds(0, nc)], sem_ref.at[i % slots])

    for i in range(min(slots, len(pieces))):
        copy(i).start()
    for i, (src, dst, scale) in enumerate(pieces):
        nr, nc = src.shape
        copy(i).wait()
        piece = stage_ref[i % slots, 0:nr, 0:nc]
        if scale is not None:
            piece = piece * scale
        dst[...] = pltpu.bitcast(piece.astype(_BF16), _U32)
        if i + slots < len(pieces):
            copy(i + slots).start()


def _emit_interleaved(schedule):
    live = list(schedule)
    tick = 0
    while live:
        for item in list(live):
            prog, first, period = item
            if tick >= first and (tick - first) % period == 0:
                if next(prog, StopIteration) is StopIteration:
                    live.remove(item)
        tick += 1


def _window_sums(zext, w):
    s = zext
    k = 1
    while k < min(w, SUBLANES):
        s = s + pltpu.roll(s, k, axis=0)
        k *= 2
    win = s[HALO:, :]
    if w > SUBLANES:
        assert w == 2 * SUBLANES
        win = win + s[HALO - SUBLANES:-SUBLANES, :]
    return win


def _ffn_chunk(h, f, c, wt):
    a = _wdot(h, wt.w_ff1[:, c:c + FF_CHUNK])
    a = jnp.square(jnp.maximum(a, 0.0)).astype(_BF16)
    part = _wdot(a, wt.w_ff2[c // 2:(c + FF_CHUNK) // 2, :])
    return part if f is None else f + part


def _front_stages(row0, seq_row0, layer, x_ref, sm, w_s_ref, b_s_ref, wt, zbuf_ref):
    tm = SUB_TILE
    d = x_ref.shape[1]
    rows = slice(row0, row0 + tm)
    zrows = slice(HALO + row0, HALO + row0 + tm)
    lyr = slice(layer, layer + 1)

    x = x_ref[rows, :]
    h = _rms(x, sm.pre_mix_g[lyr, :]).astype(_BF16)
    yield

    def proj(lo, hi, scale=1.0):
        return _wdot(h, wt.w_in[:, lo:hi]) + scale * sm.b_in[lyr, lo:hi]

    def half_proj(lo, hi):
        return proj(lo, hi, ACT_WEIGHT_SCALE)

    zbuf_ref[zrows, :] = proj(0, d)
    yield
    v = _gelu_tanh_of_half(half_proj(2 * d, 3 * d))
    yield

    t = seq_row0 + row0 + lax.broadcasted_iota(jnp.int32, (tm, LANES), 0)
    gdim = d // len(POOL_WINDOWS)
    pooled = []
    for k, w in enumerate(POOL_WINDOWS):
        zext = zbuf_ref[row0:row0 + HALO + tm, k * gdim:(k + 1) * gdim]
        win = _window_sums(zext, w)
        inv_cnt = 1.0 / jnp.minimum(t + 1, w).astype(_F32)
        inv_cnt = jnp.concatenate([inv_cnt] * (gdim // LANES), axis=1)
        pooled.append((win * inv_cnt - zext[HALO:, :]).astype(_BF16))
    yield
    gate_a = _sigmoid_of_half(half_proj(3 * d, 4 * d))
    yield
    pm = jnp.concatenate([_wdot(pooled[k], wt.pool_w[k]) for k in range(len(POOL_WINDOWS))], axis=-1)
    pm = (pm * sm.pool_scale[lyr, :]).astype(_BF16)
    merged = gate_a * _wdot(pm, wt.w_pa[...])
    yield

    mu = jnp.mean(v, axis=-1, keepdims=True)
    vc = v - mu
    vn = (vc * lax.rsqrt(jnp.mean(vc * vc, axis=-1, keepdims=True) + EPS) * sm.ln_g[lyr, :]
          + sm.ln_b[lyr, :]).astype(_BF16)
    yield
    u = _gelu_tanh_of_half(half_proj(d, 2 * d))
    yield
    hd = d // SGU_GROUPS
    n_chunks = tm // SGU_CHUNK
    causal = (lax.broadcasted_iota(jnp.int32, (SGU_CHUNK, SGU_CHUNK), 0)
              >= lax.broadcasted_iota(jnp.int32, (SGU_CHUNK, SGU_CHUNK), 1))
    blocks = [[] for _ in range(n_chunks)]
    for g in range(SGU_GROUPS):
        ws = jnp.where(causal, w_s_ref[g], 0.0).astype(_BF16)
        rhs = jnp.concatenate([vn[c * SGU_CHUNK:(c + 1) * SGU_CHUNK, g * hd:(g + 1) * hd]
                               for c in range(n_chunks)], axis=1)
        out = _dot(ws, rhs)
        for c in range(n_chunks):
            blocks[c].append(out[:, c * hd:(c + 1) * hd])
    spatial = jnp.concatenate([jnp.concatenate(b, axis=1) + b_s_ref[...] for b in blocks], axis=0)
    yield
    gate_b = _sigmoid_of_half(half_proj(4 * d, 5 * d))
    yield
    merged = merged + gate_b * _wdot((u * spatial).astype(_BF16), wt.w_pb[...])
    yield
    o = _wdot(merged.astype(_BF16), wt.w_o[...])
    yield
    x = x + _rms(o, sm.post_mix_g[lyr, :])
    h = _rms(x, sm.pre_ffn_g[lyr, :]).astype(_BF16)
    yield
    f = None
    for c in range(0, FRONT_FF_CHUNKS * FF_CHUNK, FF_CHUNK):
        f = _ffn_chunk(h, f, c, wt)
        yield
    return x, h, f


def _back_stages(x, h, f, p_bf16, layer, sm, wt):
    lyr = slice(layer, layer + 1)
    for c in range(FRONT_FF_CHUNKS * FF_CHUNK, wt.w_ff1.shape[1], FF_CHUNK):
        f = _ffn_chunk(h, f, c, wt)
        yield
    x = x + _rms(f, sm.post_ffn_g[lyr, :])
    yield
    gate = _sigmoid_of_half(_wdot(x.astype(_BF16), wt.w_gate[...]))
    e = _wdot(p_bf16, wt.w_proj[...])
    yield
    return x + _rms(gate * e, sm.post_ple_g[lyr, :])


def _layer_kernel(*refs, layer, tiles_per_seq, n_tiles):
    refs = list(refs)
    x_ref, p_ref, p_prev_ref = refs[:3]
    sm = SimpleNamespace(**dict(zip(_SMALL, refs[3:])))
    k = 3 + len(_SMALL)
    w_s_ref, b_s_ref = refs[k:k + 2]
    hbm = refs[k + 2:k + 2 + len(_BIG)]
    out_hbm = refs[k + 2 + len(_BIG)]
    packed = refs[k + 3 + len(_BIG):k + 3 + 2 * len(_BIG)]
    wt = SimpleNamespace(**dict(zip(_BIG, packed)))
    zbuf_ref, carry_x, carry_h, carry_f, stage_ref, osem_ref, wsem_ref = refs[k + 3 + 2 * len(_BIG):]

    tm, d = x_ref.shape
    assert stage_ref.shape == (STAGE_SLOTS, SUB_TILE, d) and STAGE_SLOTS == 4
    step = pl.program_id(0)
    j = step % tiles_per_seq
    par = step % 2

    def out_slot(parity, which):
        return 2 * parity + which

    def out_copy(parity, which, sub_tile):
        row = pl.multiple_of(sub_tile * SUB_TILE, SUB_TILE)
        slot = out_slot(parity, which)
        return pltpu.make_async_copy(stage_ref.at[slot], out_hbm.at[pl.ds(row, SUB_TILE), :],
                                     osem_ref.at[slot])

    @pl.when(step == 0)
    def _():
        pieces = []
        for name, src, dst in zip(_BIG, hbm, packed):
            if name == "pool_w":
                for g in range(len(POOL_WINDOWS)):
                    pieces += _matrix_pieces(src.at[layer, g], dst.at[g])
            else:
                halved_from = {"w_in": x_ref.shape[1], "w_gate": 0}.get(name)
                pieces += _matrix_pieces(src.at[layer], dst, halved_from)
        _load_weights(pieces, stage_ref, wsem_ref)
        carry_x[...] = jnp.zeros(carry_x.shape, carry_x.dtype)
        carry_h[...] = jnp.zeros(carry_h.shape, carry_h.dtype)
        carry_f[...] = jnp.zeros(carry_f.shape, carry_f.dtype)

    @pl.when(step >= 2)
    def _():
        out_copy(par, 1, 2 * (step - 2)).wait()

    @pl.when(step >= 3)
    def _():
        out_copy(par, 0, 2 * (step - 2) - 1).wait()

    def carried_program():
        state = (carry_x[...], carry_h[...], carry_f[...], p_prev_ref[...].astype(_BF16))
        stage_ref[out_slot(par, 0)] = yield from _back_stages(*state, layer, sm, wt)

    def lead_program():
        x, h, f = yield from _front_stages(0, j * tm, layer, x_ref, sm, w_s_ref, b_s_ref, wt, zbuf_ref)
        p_bf16 = p_ref[0:SUB_TILE, :].astype(_BF16)
        stage_ref[out_slot(par, 1)] = yield from _back_stages(x, h, f, p_bf16, layer, sm, wt)

    def trail_program():
        x, h, f = yield from _front_stages(SUB_TILE, j * tm, layer, x_ref, sm, w_s_ref, b_s_ref,
                                           wt, zbuf_ref)
        carry_x[...] = x
        carry_h[...] = h
        carry_f[...] = f

    @pl.when(step < n_tiles)
    def _():
        @pl.when(j == 0)
        def _():
            zbuf_ref[0:HALO, :] = jnp.zeros((HALO, d), _F32)

        @pl.when(j != 0)
        def _():
            zbuf_ref[0:HALO, :] = zbuf_ref[tm:tm + HALO, :]

        _emit_interleaved([(carried_program(), 0, 2), (lead_program(), 0, 1), (trail_program(), 4, 1)])
        out_copy(par, 1, 2 * step).start()

        @pl.when(step >= 1)
        def _():
            out_copy(par, 0, 2 * step - 1).start()

    @pl.when(step == n_tiles)
    def _():
        _emit_interleaved([(carried_program(), 0, 1)])
        out_copy(par, 0, 2 * step - 1).start()
        out_copy(par, 0, 2 * step - 1).wait()
        out_copy(1 - par, 1, 2 * (step - 1)).wait()
        out_copy(1 - par, 0, 2 * (step - 1) - 1).wait()


def _whole(a):
    nd = a.ndim
    return pl.BlockSpec(a.shape, lambda i: (0,) * nd, pipeline_mode=pl.Buffered(1))


def _of_layer(a, layer):
    nd = a.ndim
    return pl.BlockSpec((None,) + a.shape[1:], lambda i: (layer,) + (0,) * (nd - 1),
                        pipeline_mode=pl.Buffered(1))


def _layer_call(x, p, layer, seq, small, w_s, b_s_tile, big):
    n, d = x.shape
    tm = ROW_TILE
    n_tiles = n // tm
    assert n_tiles >= 2
    last = n_tiles - 1
    row = pl.BlockSpec((tm, d), lambda i: (jnp.minimum(i, last), 0))
    prow = pl.BlockSpec((None, tm, p.shape[2]), lambda i: (layer, jnp.minimum(i, last), 0))
    prow_prev = pl.BlockSpec((None, SUB_TILE, p.shape[2]),
                             lambda i: (layer, jnp.maximum(2 * i - 1, 0), 0))
    packed = [pltpu.VMEM(w.shape[1:-2] + (w.shape[-2] // 2, w.shape[-1]), _U32) for w in big]
    return pl.pallas_call(
        functools.partial(_layer_kernel, layer=layer, tiles_per_seq=seq // tm, n_tiles=n_tiles),
        out_shape=jax.ShapeDtypeStruct((n, d), _F32),
        grid=(n_tiles + 1,),
        in_specs=([row, prow, prow_prev] + [_whole(a) for a in small]
                  + [_of_layer(w_s, layer), _of_layer(b_s_tile, layer)]
                  + [pl.BlockSpec(memory_space=pl.ANY)] * len(big)),
        out_specs=pl.BlockSpec(memory_space=pl.ANY),
        scratch_shapes=packed + [
            pltpu.VMEM((HALO + tm, d), _F32),
            pltpu.VMEM((SUB_TILE, d), _F32),
            pltpu.VMEM((SUB_TILE, d), _BF16),
            pltpu.VMEM((SUB_TILE, d), _F32),
            pltpu.VMEM((STAGE_SLOTS, STAGE_ROWS, STAGE_COLS), _F32),
            pltpu.SemaphoreType.DMA((STAGE_SLOTS,)),
            pltpu.SemaphoreType.DMA((STAGE_SLOTS,))],
        compiler_params=pltpu.CompilerParams(
            dimension_semantics=("arbitrary",), vmem_limit_bytes=VMEM_LIMIT_BYTES),
        name="layer",
    )(x, p, p, *small, w_s, b_s_tile, *big)


def kernel(x, p, pre_mix_g, w_in, b_in, pool_w, pool_scale, sgu_ln_g, sgu_ln_b, sgu_w_s,
           sgu_b_s, w_pa, w_pb, w_o, post_mix_g, pre_ffn_g, w_ff1, w_ff2, post_ffn_g,
           w_ple_gate, w_ple_proj, post_ple_g):
    b, s, d = x.shape
    depth = w_in.shape[0]
    assert s % ROW_TILE == 0 and SUB_TILE % SGU_CHUNK == 0
    assert d % SGU_GROUPS == 0 and (d // len(POOL_WINDOWS)) % LANES == 0
    assert w_ff1.shape[2] % FF_CHUNK == 0
    xf = x.reshape(b * s, d)
    pf = p.reshape(depth, b * s, p.shape[-1])
    b_s_tile = jnp.repeat(jnp.swapaxes(sgu_b_s, 1, 2), d // SGU_GROUPS, axis=2)
    small = (pre_mix_g, b_in, pool_scale, sgu_ln_g, sgu_ln_b, post_mix_g,
             pre_ffn_g, post_ffn_g, post_ple_g)
    big = (w_in, pool_w, w_pa, w_pb, w_o, w_ff1, w_ff2, w_ple_gate, w_ple_proj)
    for i in range(depth):
        xf = _layer_call(xf, pf, i, s, small, sgu_w_s, b_s_tile, big)
    return xf.reshape(b, s, d)
```

```python
import functools
from types import SimpleNamespace

import jax
import jax.numpy as jnp
from jax import lax
from jax.experimental import pallas as pl
from jax.experimental.pallas import tpu as pltpu

POOL_WINDOWS = (2, 4, 8, 16)
SUBLANES, LANES = 8, 128
HALO = 16
SGU_CHUNK = 128
SGU_GROUPS = 8
EPS = 1e-6
FF_CHUNK = 1024
FRONT_FF_CHUNKS = 2
SUB_TILE = 256
ROW_TILE = 2 * SUB_TILE
STAGE_ROWS, STAGE_COLS = SUB_TILE, 1024
STAGE_SLOTS = 4
VMEM_LIMIT_BYTES = 60 * 1024 * 1024

_BF16 = jnp.bfloat16
_F32 = jnp.float32
_U32 = jnp.uint32

_SMALL = ("pre_mix_g", "b_in", "pool_scale", "ln_g", "ln_b", "post_mix_g",
          "pre_ffn_g", "post_ffn_g", "post_ple_g")
_BIG = ("w_in", "pool_w", "w_pa", "w_pb", "w_o", "w_ff1", "w_ff2", "w_gate", "w_proj")


def _rms(x, g):
    return x * lax.rsqrt(jnp.mean(x * x, axis=-1, keepdims=True) + EPS) * g


def _dot(a, b):
    return jnp.dot(a, b, preferred_element_type=_F32)


def _wdot(a, w_packed):
    return _dot(a, pltpu.bitcast(w_packed, _BF16))


ACT_WEIGHT_SCALE = 0.5
_GELU_K1 = 2.0 * 0.7978845608028654
_GELU_K2 = 8.0 * 0.7978845608028654 * 0.044715


def _twice_sigmoid_of_half(half_x):
    return jnp.tanh(half_x) + 1.0


def _gelu_tanh_of_half(half_x):
    return half_x * jnp.tanh(half_x * (_GELU_K1 + _GELU_K2 * (half_x * half_x))) + half_x


def _matrix_pieces(src, dst, halved_from_col=None):
    k, n = src.shape
    pieces = []
    for r in range(0, k, STAGE_ROWS):
        nr = min(STAGE_ROWS, k - r)
        for c in range(0, n, STAGE_COLS):
            nc = min(STAGE_COLS, n - c)
            halved = halved_from_col is not None and c >= halved_from_col
            assert halved_from_col is None or halved or c + nc <= halved_from_col
            pieces.append((src.at[pl.ds(r, nr), pl.ds(c, nc)],
                           dst.at[pl.ds(r // 2, nr // 2), pl.ds(c, nc)],
                           ACT_WEIGHT_SCALE if halved else None))
    return pieces


def _load_weights(pieces, stage_ref, sem_ref):
    slots = stage_ref.shape[0]

    def copy(i):
        src = pieces[i][0]
        nr, nc = src.shape
        return pltpu.make_async_copy(
            src, stage_ref.at[i % slots, pl.ds(0, nr), pl.ds(0, nc)], sem_ref.at[i % slots])

    for i in range(min(slots, len(pieces))):
        copy(i).start()
    for i, (src, dst, scale) in enumerate(pieces):
        nr, nc = src.shape
        copy(i).wait()
        piece = stage_ref[i % slots, 0:nr, 0:nc]
        if scale is not None:
            piece = piece * scale
        dst[...] = pltpu.bitcast(piece.astype(_BF16), _U32)
        if i + slots < len(pieces):
            copy(i + slots).start()


def _emit_interleaved(schedule):
    live = list(schedule)
    tick = 0
    while live:
        for item in list(live):
            prog, first, period = item
            if tick >= first and (tick - first) % period == 0:
                if next(prog, StopIteration) is StopIteration:
                    live.remove(item)
        tick += 1


def _window_sums(zext, w):
    s = zext
    k = 1
    while k < min(w, SUBLANES):
        s = s + pltpu.roll(s, k, axis=0)
        k *= 2
    win = s[HALO:, :]
    if w > SUBLANES:
        assert w == 2 * SUBLANES
        win = win + s[HALO - SUBLANES:-SUBLANES, :]
    return win


def _ffn_chunk(h, f, c, wt):
    a = _wdot(h, wt.w_ff1[:, c:c + FF_CHUNK])
    a = jnp.square(jnp.maximum(a, 0.0)).astype(_BF16)
    part = _wdot(a, wt.w_ff2[c // 2:(c + FF_CHUNK) // 2, :])
    return part if f is None else f + part


def _front_stages(row0, seq_row0, layer, x_ref, sm, w_s_ref, b_s_ref, wt, zbuf_ref):
    tm = SUB_TILE
    d = x_ref.shape[1]
    rows = slice(row0, row0 + tm)
    zrows = slice(HALO + row0, HALO + row0 + tm)
    lyr = slice(layer, layer + 1)

    x = x_ref[rows, :]
    h = _rms(x, sm.pre_mix_g[lyr, :]).astype(_BF16)
    yield

    def proj(lo, hi, scale=1.0):
        return _wdot(h, wt.w_in[:, lo:hi]) + scale * sm.b_in[lyr, lo:hi]

    def half_proj(lo, hi):
        return proj(lo, hi, ACT_WEIGHT_SCALE)

    zbuf_ref[zrows, :] = proj(0, d)
    yield
    v = _gelu_tanh_of_half(half_proj(2 * d, 3 * d))
    yield

    t = seq_row0 + row0 + lax.broadcasted_iota(jnp.int32, (tm, LANES), 0)
    gdim = d // len(POOL_WINDOWS)
    pooled = []
    for k, w in enumerate(POOL_WINDOWS):
        zext = zbuf_ref[row0:row0 + HALO + tm, k * gdim:(k + 1) * gdim]
        win = _window_sums(zext, w)
        inv_cnt = 1.0 / jnp.minimum(t + 1, w).astype(_F32)
        inv_cnt = jnp.concatenate([inv_cnt] * (gdim // LANES), axis=1)
        pooled.append((win * inv_cnt - zext[HALO:, :]).astype(_BF16))
    yield
    gate_a = _twice_sigmoid_of_half(half_proj(3 * d, 4 * d))
    yield
    pm = jnp.concatenate([_wdot(pooled[k], wt.pool_w[k]) for k in range(len(POOL_WINDOWS))], axis=-1)
    pm = (pm * sm.pool_scale[lyr, :]).astype(_BF16)
    merged = gate_a * _wdot(pm, wt.w_pa[...])
    yield

    mu = jnp.mean(v, axis=-1, keepdims=True)
    vc = v - mu
    vn = (vc * lax.rsqrt(jnp.mean(vc * vc, axis=-1, keepdims=True) + EPS) * sm.ln_g[lyr, :]
          + sm.ln_b[lyr, :]).astype(_BF16)
    yield
    u = _gelu_tanh_of_half(half_proj(d, 2 * d))
    yield
    hd = d // SGU_GROUPS
    n_chunks = tm // SGU_CHUNK
    causal = (lax.broadcasted_iota(jnp.int32, (SGU_CHUNK, SGU_CHUNK), 0)
              >= lax.broadcasted_iota(jnp.int32, (SGU_CHUNK, SGU_CHUNK), 1))
    blocks = [[] for _ in range(n_chunks)]
    for g in range(SGU_GROUPS):
        ws = jnp.where(causal, w_s_ref[g], 0.0).astype(_BF16)
        rhs = jnp.concatenate([vn[c * SGU_CHUNK:(c + 1) * SGU_CHUNK, g * hd:(g + 1) * hd]
                               for c in range(n_chunks)], axis=1)
        out = _dot(ws, rhs)
        for c in range(n_chunks):
            blocks[c].append(out[:, c * hd:(c + 1) * hd])
    spatial = jnp.concatenate([jnp.concatenate(b, axis=1) + b_s_ref[...] for b in blocks], axis=0)
    yield
    gate_b = _twice_sigmoid_of_half(half_proj(4 * d, 5 * d))
    yield
    merged = merged + gate_b * _wdot((u * spatial).astype(_BF16), wt.w_pb[...])
    yield
    o = _wdot(merged.astype(_BF16), wt.w_o[...])
    yield
    x = x + _rms(o, sm.post_mix_g[lyr, :])
    h = _rms(x, sm.pre_ffn_g[lyr, :]).astype(_BF16)
    yield
    f = None
    for c in range(0, FRONT_FF_CHUNKS * FF_CHUNK, FF_CHUNK):
        f = _ffn_chunk(h, f, c, wt)
        yield
    return x, h, f


def _back_stages(x, h, f, p_bf16, layer, sm, wt):
    lyr = slice(layer, layer + 1)
    for c in range(FRONT_FF_CHUNKS * FF_CHUNK, wt.w_ff1.shape[1], FF_CHUNK):
        f = _ffn_chunk(h, f, c, wt)
        yield
    x = x + _rms(f, sm.post_ffn_g[lyr, :])
    yield
    gate = _twice_sigmoid_of_half(_wdot(x.astype(_BF16), wt.w_gate[...]))
    e = _wdot(p_bf16, wt.w_proj[...])
    yield
    return x + _rms(gate * e, sm.post_ple_g[lyr, :])


def _layer_kernel(*refs, layer, tiles_per_seq, n_tiles):
    refs = list(refs)
    x_ref, p_ref, p_prev_ref = refs[:3]
    sm = SimpleNamespace(**dict(zip(_SMALL, refs[3:])))
    k = 3 + len(_SMALL)
    w_s_ref, b_s_ref = refs[k:k + 2]
    hbm = refs[k + 2:k + 2 + len(_BIG)]
    out_hbm = refs[k + 2 + len(_BIG)]
    packed = refs[k + 3 + len(_BIG):k + 3 + 2 * len(_BIG)]
    wt = SimpleNamespace(**dict(zip(_BIG, packed)))
    zbuf_ref, carry_x, carry_h, carry_f, stage_ref, osem_ref, wsem_ref = refs[k + 3 + 2 * len(_BIG):]

    tm, d = x_ref.shape
    assert stage_ref.shape == (STAGE_SLOTS, SUB_TILE, d) and STAGE_SLOTS == 4
    step = pl.program_id(0)
    j = step % tiles_per_seq
    par = step % 2

    def out_slot(parity, which):
        return 2 * parity + which

    def out_copy(parity, which, sub_tile):
        row = pl.multiple_of(sub_tile * SUB_TILE, SUB_TILE)
        slot = out_slot(parity, which)
        return pltpu.make_async_copy(stage_ref.at[slot], out_hbm.at[pl.ds(row, SUB_TILE), :],
                                     osem_ref.at[slot])

    @pl.when(step == 0)
    def _():
        pieces = []
        for name, src, dst in zip(_BIG, hbm, packed):
            if name == "pool_w":
                for g in range(len(POOL_WINDOWS)):
                    pieces += _matrix_pieces(src.at[layer, g], dst.at[g])
            else:
                halved_from = {"w_in": x_ref.shape[1], "w_gate": 0,
                               "w_pa": 0, "w_pb": 0, "w_proj": 0}.get(name)
                pieces += _matrix_pieces(src.at[layer], dst, halved_from)
        _load_weights(pieces, stage_ref, wsem_ref)
        carry_x[...] = jnp.zeros(carry_x.shape, carry_x.dtype)
        carry_h[...] = jnp.zeros(carry_h.shape, carry_h.dtype)
        carry_f[...] = jnp.zeros(carry_f.shape, carry_f.dtype)

    @pl.when(step >= 2)
    def _():
        out_copy(par, 1, 2 * (step - 2)).wait()

    @pl.when(step >= 3)
    def _():
        out_copy(par, 0, 2 * (step - 2) - 1).wait()

    def carried_program():
        state = (carry_x[...], carry_h[...], carry_f[...], p_prev_ref[...].astype(_BF16))
        stage_ref[out_slot(par, 0)] = yield from _back_stages(*state, layer, sm, wt)

    def lead_program():
        x, h, f = yield from _front_stages(0, j * tm, layer, x_ref, sm, w_s_ref, b_s_ref, wt, zbuf_ref)
        p_bf16 = p_ref[0:SUB_TILE, :].astype(_BF16)
        stage_ref[out_slot(par, 1)] = yield from _back_stages(x, h, f, p_bf16, layer, sm, wt)

    def trail_program():
        x, h, f = yield from _front_stages(SUB_TILE, j * tm, layer, x_ref, sm, w_s_ref, b_s_ref,
                                           wt, zbuf_ref)
        carry_x[...] = x
        carry_h[...] = h
        carry_f[...] = f

    @pl.when(step < n_tiles)
    def _():
        @pl.when(j == 0)
        def _():
            zbuf_ref[0:HALO, :] = jnp.zeros((HALO, d), _F32)

        @pl.when(j != 0)
        def _():
            zbuf_ref[0:HALO, :] = zbuf_ref[tm:tm + HALO, :]

        _emit_interleaved([(carried_program(), 0, 2), (lead_program(), 0, 1), (trail_program(), 4, 1)])
        out_copy(par, 1, 2 * step).start()

        @pl.when(step >= 1)
        def _():
            out_copy(par, 0, 2 * step - 1).start()

    @pl.when(step == n_tiles)
    def _():
        _emit_interleaved([(carried_program(), 0, 1)])
        out_copy(par, 0, 2 * step - 1).start()
        out_copy(par, 0, 2 * step - 1).wait()
        out_copy(1 - par, 1, 2 * (step - 1)).wait()
        out_copy(1 - par, 0, 2 * (step - 1) - 1).wait()


def _whole(a):
    nd = a.ndim
    return pl.BlockSpec(a.shape, lambda i: (0,) * nd, pipeline_mode=pl.Buffered(1))


def _of_layer(a, layer):
    nd = a.ndim
    return pl.BlockSpec((None,) + a.shape[1:], lambda i: (layer,) + (0,) * (nd - 1),
                        pipeline_mode=pl.Buffered(1))


def _layer_call(x, p, layer, seq, small, w_s, b_s_tile, big):
    n, d = x.shape
    tm = ROW_TILE
    n_tiles = n // tm
    assert n_tiles >= 2
    last = n_tiles - 1
    row = pl.BlockSpec((tm, d), lambda i: (jnp.minimum(i, last), 0))
    prow = pl.BlockSpec((None, tm, p.shape[2]), lambda i: (layer, jnp.minimum(i, last), 0))
    prow_prev = pl.BlockSpec((None, SUB_TILE, p.shape[2]),
                             lambda i: (layer, jnp.maximum(2 * i - 1, 0), 0))
    packed = [pltpu.VMEM(w.shape[1:-2] + (w.shape[-2] // 2, w.shape[-1]), _U32) for w in big]
    return pl.pallas_call(
        functools.partial(_layer_kernel, layer=layer, tiles_per_seq=seq // tm, n_tiles=n_tiles),
        out_shape=jax.ShapeDtypeStruct((n, d), _F32),
        grid=(n_tiles + 1,),
        in_specs=([row, prow, prow_prev] + [_whole(a) for a in small]
                  + [_of_layer(w_s, layer), _of_layer(b_s_tile, layer)]
                  + [pl.BlockSpec(memory_space=pl.ANY)] * len(big)),
        out_specs=pl.BlockSpec(memory_space=pl.ANY),
        scratch_shapes=packed + [
            pltpu.VMEM((HALO + tm, d), _F32),
            pltpu.VMEM((SUB_TILE, d), _F32),
            pltpu.VMEM((SUB_TILE, d), _BF16),
            pltpu.VMEM((SUB_TILE, d), _F32),
            pltpu.VMEM((STAGE_SLOTS, STAGE_ROWS, STAGE_COLS), _F32),
            pltpu.SemaphoreType.DMA((STAGE_SLOTS,)),
            pltpu.SemaphoreType.DMA((STAGE_SLOTS,))],
        compiler_params=pltpu.CompilerParams(
            dimension_semantics=("arbitrary",), vmem_limit_bytes=VMEM_LIMIT_BYTES),
        name="layer",
    )(x, p, p, *small, w_s, b_s_tile, *big)


def kernel(x, p, pre_mix_g, w_in, b_in, pool_w, pool_scale, sgu_ln_g, sgu_ln_b, sgu_w_s,
           sgu_b_s, w_pa, w_pb, w_o, post_mix_g, pre_ffn_g, w_ff1, w_ff2, post_ffn_g,
           w_ple_gate, w_ple_proj, post_ple_g):
    b, s, d = x.shape
    depth = w_in.shape[0]
    assert s % ROW_TILE == 0 and SUB_TILE % SGU_CHUNK == 0
    assert d % SGU_GROUPS == 0 and (d // len(POOL_WINDOWS)) % LANES == 0
    assert w_ff1.shape[2] % FF_CHUNK == 0
    xf = x.reshape(b * s, d)
    pf = p.reshape(depth, b * s, p.shape[-1])
    b_s_tile = jnp.repeat(jnp.swapaxes(sgu_b_s, 1, 2), d // SGU_GROUPS, axis=2)
    small = (pre_mix_g, b_in, pool_scale, sgu_ln_g, sgu_ln_b, post_mix_g,
             pre_ffn_g, post_ffn_g, post_ple_g)
    big = (w_in, pool_w, w_pa, w_pb, w_o, w_ff1, w_ff2, w_ple_gate, w_ple_proj)
    for i in range(depth):
        xf = _layer_call(xf, pf, i, s, small, sgu_w_s, b_s_tile, big)
    return xf.reshape(b, s, d)
```

```python
import functools
from types import SimpleNamespace

import jax
import jax.numpy as jnp
from jax import lax
from jax.experimental import pallas as pl
from jax.experimental.pallas import tpu as pltpu

POOL_WINDOWS = (2, 4, 8, 16)
SUBLANES, LANES = 8, 128
HALO = 16
SGU_CHUNK = 128
SGU_GROUPS = 8
EPS = 1e-6
FF_CHUNK = 1024
FRONT_FF_CHUNKS = 2
SUB_TILE = 256
ROW_TILE = 2 * SUB_TILE
STAGE_ROWS, STAGE_COLS = SUB_TILE, 1024
STAGE_SLOTS = 4
VMEM_LIMIT_BYTES = 60 * 1024 * 1024

_BF16 = jnp.bfloat16
_F32 = jnp.float32
_U32 = jnp.uint32

_SMALL = ("pre_mix_g", "b_in", "pool_scale", "ln_g", "ln_b", "post_mix_g",
          "pre_ffn_g", "post_ffn_g", "post_ple_g")
_BIG = ("w_in", "pool_w", "w_pa", "w_pb", "w_o", "w_ff1", "w_ff2", "w_gate", "w_proj")


def _rms(x, g):
    return x * lax.rsqrt(jnp.mean(x * x, axis=-1, keepdims=True) + EPS) * g


def _dot(a, b):
    return jnp.dot(a, b, preferred_element_type=_F32)


def _wdot(a, w_packed):
    return _dot(a, pltpu.bitcast(w_packed, _BF16))


ACT_WEIGHT_SCALE = 0.5
_GELU_K1 = 2.0 * 0.7978845608028654
_GELU_K2 = 8.0 * 0.7978845608028654 * 0.044715


def _sigmoid_of_half(half_x):
    return 0.5 * jnp.tanh(half_x) + 0.5


def _gelu_tanh_of_half(half_x):
    return half_x * jnp.tanh(half_x * (_GELU_K1 + _GELU_K2 * (half_x * half_x))) + half_x


def _matrix_pieces(src, dst, halved_from_col=None):
    k, n = src.shape
    pieces = []
    for r in range(0, k, STAGE_ROWS):
        nr = min(STAGE_ROWS, k - r)
        for c in range(0, n, STAGE_COLS):
            nc = min(STAGE_COLS, n - c)
            halved = halved_from_col is not None and c >= halved_from_col
            assert halved_from_col is None or halved or c + nc <= halved_from_col
            pieces.append((src.at[pl.ds(r, nr), pl.ds(c, nc)],
                           dst.at[pl.ds(r // 2, nr // 2), pl.ds(c, nc)],
                           ACT_WEIGHT_SCALE if halved else None))
    return pieces


def _load_weights(pieces, stage_ref, sem_ref):
    slots = stage_ref.shape[0]

    def copy(i):
        src = pieces[i][0]
        nr, nc = src.shape
        return pltpu.make_async_copy(
            src, stage_ref.at[i % slots, pl.ds(0, nr), pl.ds(0, nc)], sem_ref.at[i % slots])

    for i in range(min(slots, len(pieces))):
        copy(i).start()
    for i, (src, dst, scale) in enumerate(pieces):
        nr, nc = src.shape
        copy(i).wait()
        piece = stage_ref[i % slots, 0:nr, 0:nc]
        if scale is not None:
            piece = piece * scale
        dst[...] = pltpu.bitcast(piece.astype(_BF16), _U32)
        if i + slots < len(pieces):
            copy(i + slots).start()


def _emit_interleaved(schedule):
    live = list(schedule)
    tick = 0
    while live:
        for item in list(live):
            prog, first, period = item
            if tick >= first and (tick - first) % period == 0:
                if next(prog, StopIteration) is StopIteration:
                    live.remove(item)
        tick += 1


def _window_sums(zext, w):
    s = zext
    k = 1
    while k < min(w, SUBLANES):
        s = s + pltpu.roll(s, k, axis=0)
        k *= 2
    win = s[HALO:, :]
    if w > SUBLANES:
        assert w == 2 * SUBLANES
        win = win + s[HALO - SUBLANES:-SUBLANES, :]
    return win


def _ffn_chunk(h, f, c, wt):
    a = _wdot(h, wt.w_ff1[:, c:c + FF_CHUNK])
    a = jnp.square(jnp.maximum(a, 0.0)).astype(_BF16)
    part = _wdot(a, wt.w_ff2[c // 2:(c + FF_CHUNK) // 2, :])
    return part if f is None else f + part


def _front_stages(row0, seq_row0, layer, x_ref, sm, w_s_ref, b_s_ref, wt, zbuf_ref):
    tm = SUB_TILE
    d = x_ref.shape[1]
    rows = slice(row0, row0 + tm)
    zrows = slice(HALO + row0, HALO + row0 + tm)
    lyr = slice(layer, layer + 1)

    x = x_ref[rows, :]
    h = _rms(x, sm.pre_mix_g[lyr, :]).astype(_BF16)
    yield

    def proj(lo, hi, scale=1.0):
        return _wdot(h, wt.w_in[:, lo:hi]) + scale * sm.b_in[lyr, lo:hi]

    def half_proj(lo, hi):
        return proj(lo, hi, ACT_WEIGHT_SCALE)

    zbuf_ref[zrows, :] = proj(0, d)
    yield
    v = _gelu_tanh_of_half(half_proj(2 * d, 3 * d))
    yield

    t = seq_row0 + row0 + lax.broadcasted_iota(jnp.int32, (tm, LANES), 0)
    gdim = d // len(POOL_WINDOWS)
    pooled = []
    for k, w in enumerate(POOL_WINDOWS):
        zext = zbuf_ref[row0:row0 + HALO + tm, k * gdim:(k + 1) * gdim]
        win = _window_sums(zext, w)
        inv_cnt = 1.0 / jnp.minimum(t + 1, w).astype(_F32)
        inv_cnt = jnp.concatenate([inv_cnt] * (gdim // LANES), axis=1)
        pooled.append((win * inv_cnt - zext[HALO:, :]).astype(_BF16))
    yield
    gate_a = _sigmoid_of_half(half_proj(3 * d, 4 * d))
    yield
    pm = jnp.concatenate([_wdot(pooled[k], wt.pool_w[k]) for k in range(len(POOL_WINDOWS))], axis=-1)
    pm = (pm * sm.pool_scale[lyr, :]).astype(_BF16)
    merged = gate_a * _wdot(pm, wt.w_pa[...])
    yield

    mu = jnp.mean(v, axis=-1, keepdims=True)
    vc = v - mu
    vn = (vc * lax.rsqrt(jnp.mean(vc * vc, axis=-1, keepdims=True) + EPS) * sm.ln_g[lyr, :]
          + sm.ln_b[lyr, :]).astype(_BF16)
    yield
    u = _gelu_tanh_of_half(half_proj(d, 2 * d))
    yield
    hd = d // SGU_GROUPS
    n_chunks = tm // SGU_CHUNK
    causal = (lax.broadcasted_iota(jnp.int32, (SGU_CHUNK, SGU_CHUNK), 0)
              >= lax.broadcasted_iota(jnp.int32, (SGU_CHUNK, SGU_CHUNK), 1))
    blocks = [[] for _ in range(n_chunks)]
    for g in range(SGU_GROUPS):
        ws = jnp.where(causal, w_s_ref[g], 0.0).astype(_BF16)
        rhs = jnp.concatenate([vn[c * SGU_CHUNK:(c + 1) * SGU_CHUNK, g * hd:(g + 1) * hd]
                               for c in range(n_chunks)], axis=1)
        out = _dot(ws, rhs)
        for c in range(n_chunks):
            blocks[c].append(out[:, c * hd:(c + 1) * hd])
    spatial = jnp.concatenate([jnp.concatenate(b, axis=1) + b_s_ref[...] for b in blocks], axis=0)
    yield
    gate_b = _sigmoid_of_half(half_proj(4 * d, 5 * d))
    yield
    merged = merged + gate_b * _wdot((u * spatial).astype(_BF16), wt.w_pb[...])
    yield
    o = _wdot(merged.astype(_BF16), wt.w_o[...])
    yield
    x = x + _rms(o, sm.post_mix_g[lyr, :])
    h = _rms(x, sm.pre_ffn_g[lyr, :]).astype(_BF16)
    yield
    f = None
    for c in range(0, FRONT_FF_CHUNKS * FF_CHUNK, FF_CHUNK):
        f = _ffn_chunk(h, f, c, wt)
        yield
    return x, h, f


def _back_stages(x, h, f, p_bf16, layer, sm, wt):
    lyr = slice(layer, layer + 1)
    for c in range(FRONT_FF_CHUNKS * FF_CHUNK, wt.w_ff1.shape[1], FF_CHUNK):
        f = _ffn_chunk(h, f, c, wt)
        yield
    x = x + _rms(f, sm.post_ffn_g[lyr, :])
    yield
    gate = jnp.tanh(_wdot(x.astype(_BF16), wt.w_gate[...])) + 1.0
    e = _wdot(p_bf16, wt.w_proj[...])
    yield
    return x + _rms(gate * e, sm.post_ple_g[lyr, :])


def _layer_kernel(*refs, layer, tiles_per_seq, n_tiles):
    refs = list(refs)
    x_ref, p_ref, p_prev_ref = refs[:3]
    sm = SimpleNamespace(**dict(zip(_SMALL, refs[3:])))
    k = 3 + len(_SMALL)
    w_s_ref, b_s_ref = refs[k:k + 2]
    hbm = refs[k + 2:k + 2 + len(_BIG)]
    out_hbm = refs[k + 2 + len(_BIG)]
    packed = refs[k + 3 + len(_BIG):k + 3 + 2 * len(_BIG)]
    wt = SimpleNamespace(**dict(zip(_BIG, packed)))
    zbuf_ref, carry_x, carry_h, carry_f, stage_ref, osem_ref, wsem_ref = refs[k + 3 + 2 * len(_BIG):]

    tm, d = x_ref.shape
    assert stage_ref.shape == (STAGE_SLOTS, SUB_TILE, d) and STAGE_SLOTS == 4
    step = pl.program_id(0)
    j = step % tiles_per_seq
    par = step % 2

    def out_slot(parity, which):
        return 2 * parity + which

    def out_copy(parity, which, sub_tile):
        row = pl.multiple_of(sub_tile * SUB_TILE, SUB_TILE)
        slot = out_slot(parity, which)
        return pltpu.make_async_copy(stage_ref.at[slot], out_hbm.at[pl.ds(row, SUB_TILE), :],
                                     osem_ref.at[slot])

    @pl.when(step == 0)
    def _():
        pieces = []
        for name, src, dst in zip(_BIG, hbm, packed):
            if name == "pool_w":
                for g in range(len(POOL_WINDOWS)):
                    pieces += _matrix_pieces(src.at[layer, g], dst.at[g])
            else:
                halved_from = {"w_in": x_ref.shape[1], "w_gate": 0, "w_proj": 0}.get(name)
                pieces += _matrix_pieces(src.at[layer], dst, halved_from)
        _load_weights(pieces, stage_ref, wsem_ref)
        carry_x[...] = jnp.zeros(carry_x.shape, carry_x.dtype)
        carry_h[...] = jnp.zeros(carry_h.shape, carry_h.dtype)
        carry_f[...] = jnp.zeros(carry_f.shape, carry_f.dtype)

    @pl.when(step >= 2)
    def _():
        out_copy(par, 1, 2 * (step - 2)).wait()

    @pl.when(step >= 3)
    def _():
        out_copy(par, 0, 2 * (step - 2) - 1).wait()

    def carried_program():
        state = (carry_x[...], carry_h[...], carry_f[...], p_prev_ref[...].astype(_BF16))
        stage_ref[out_slot(par, 0)] = yield from _back_stages(*state, layer, sm, wt)

    def lead_program():
        x, h, f = yield from _front_stages(0, j * tm, layer, x_ref, sm, w_s_ref, b_s_ref, wt, zbuf_ref)
        p_bf16 = p_ref[0:SUB_TILE, :].astype(_BF16)
        stage_ref[out_slot(par, 1)] = yield from _back_stages(x, h, f, p_bf16, layer, sm, wt)

    def trail_program():
        x, h, f = yield from _front_stages(SUB_TILE, j * tm, layer, x_ref, sm, w_s_ref, b_s_ref,
                                           wt, zbuf_ref)
        carry_x[...] = x
        carry_h[...] = h
        carry_f[...] = f

    @pl.when(step < n_tiles)
    def _():
        @pl.when(j == 0)
        def _():
            zbuf_ref[0:HALO, :] = jnp.zeros((HALO, d), _F32)

        @pl.when(j != 0)
        def _():
            zbuf_ref[0:HALO, :] = zbuf_ref[tm:tm + HALO, :]

        _emit_interleaved([(carried_program(), 0, 2), (lead_program(), 0, 1), (trail_program(), 4, 1)])
        out_copy(par, 1, 2 * step).start()

        @pl.when(step >= 1)
        def _():
            out_copy(par, 0, 2 * step - 1).start()

    @pl.when(step == n_tiles)
    def _():
        _emit_interleaved([(carried_program(), 0, 1)])
        out_copy(par, 0, 2 * step - 1).start()
        out_copy(par, 0, 2 * step - 1).wait()
        out_copy(1 - par, 1, 2 * (step - 1)).wait()
        out_copy(1 - par, 0, 2 * (step - 1) - 1).wait()


def _whole(a):
    nd = a.ndim
    return pl.BlockSpec(a.shape, lambda i: (0,) * nd, pipeline_mode=pl.Buffered(1))


def _of_layer(a, layer):
    nd = a.ndim
    return pl.BlockSpec((None,) + a.shape[1:], lambda i: (layer,) + (0,) * (nd - 1),
                        pipeline_mode=pl.Buffered(1))


def _layer_call(x, p, layer, seq, small, w_s, b_s_tile, big):
    n, d = x.shape
    tm = ROW_TILE
    n_tiles = n // tm
    assert n_tiles >= 2
    last = n_tiles - 1
    row = pl.BlockSpec((tm, d), lambda i: (jnp.minimum(i, last), 0))
    prow = pl.BlockSpec((None, tm, p.shape[2]), lambda i: (layer, jnp.minimum(i, last), 0))
    prow_prev = pl.BlockSpec((None, SUB_TILE, p.shape[2]),
                             lambda i: (layer, jnp.maximum(2 * i - 1, 0), 0))
    packed = [pltpu.VMEM(w.shape[1:-2] + (w.shape[-2] // 2, w.shape[-1]), _U32) for w in big]
    return pl.pallas_call(
        functools.partial(_layer_kernel, layer=layer, tiles_per_seq=seq // tm, n_tiles=n_tiles),
        out_shape=jax.ShapeDtypeStruct((n, d), _F32),
        grid=(n_tiles + 1,),
        in_specs=([row, prow, prow_prev] + [_whole(a) for a in small]
                  + [_of_layer(w_s, layer), _of_layer(b_s_tile, layer)]
                  + [pl.BlockSpec(memory_space=pl.ANY)] * len(big)),
        out_specs=pl.BlockSpec(memory_space=pl.ANY),
        scratch_shapes=packed + [
            pltpu.VMEM((HALO + tm, d), _F32),
            pltpu.VMEM((SUB_TILE, d), _F32),
            pltpu.VMEM((SUB_TILE, d), _BF16),
            pltpu.VMEM((SUB_TILE, d), _F32),
            pltpu.VMEM((STAGE_SLOTS, STAGE_ROWS, STAGE_COLS), _F32),
            pltpu.SemaphoreType.DMA((STAGE_SLOTS,)),
            pltpu.SemaphoreType.DMA((STAGE_SLOTS,))],
        compiler_params=pltpu.CompilerParams(
            dimension_semantics=("arbitrary",), vmem_limit_bytes=VMEM_LIMIT_BYTES),
        name="layer",
    )(x, p, p, *small, w_s, b_s_tile, *big)


def kernel(x, p, pre_mix_g, w_in, b_in, pool_w, pool_scale, sgu_ln_g, sgu_ln_b, sgu_w_s,
           sgu_b_s, w_pa, w_pb, w_o, post_mix_g, pre_ffn_g, w_ff1, w_ff2, post_ffn_g,
           w_ple_gate, w_ple_proj, post_ple_g):
    b, s, d = x.shape
    depth = w_in.shape[0]
    assert s % ROW_TILE == 0 and SUB_TILE % SGU_CHUNK == 0
    assert d % SGU_GROUPS == 0 and (d // len(POOL_WINDOWS)) % LANES == 0
    assert w_ff1.shape[2] % FF_CHUNK == 0
    xf = x.reshape(b * s, d)
    pf = p.reshape(depth, b * s, p.shape[-1])
    b_s_tile = jnp.repeat(jnp.swapaxes(sgu_b_s, 1, 2), d // SGU_GROUPS, axis=2)
    small = (pre_mix_g, b_in, pool_scale, sgu_ln_g, sgu_ln_b, post_mix_g,
             pre_ffn_g, post_ffn_g, post_ple_g)
    big = (w_in, pool_w, w_pa, w_pb, w_o, w_ff1, w_ff2, w_ple_gate, w_ple_proj)
    for i in range(depth):
        xf = _layer_call(xf, pf, i, s, small, sgu_w_s, b_s_tile, big)
    return xf.reshape(b, s, d)
```

```python
import functools
from types import SimpleNamespace

import jax
import jax.numpy as jnp
from jax import lax
from jax.experimental import pallas as pl
from jax.experimental.pallas import tpu as pltpu

POOL_WINDOWS = (2, 4, 8, 16)
SUBLANES, LANES = 8, 128
HALO = 16
SGU_CHUNK = 128
SGU_GROUPS = 8
EPS = 1e-6
FF_CHUNK = 1024
FRONT_FF_CHUNKS = 2
SUB_TILE = 256
ROW_TILE = 2 * SUB_TILE
STAGE_ROWS, STAGE_COLS = SUB_TILE, 1024
STAGE_SLOTS = 4
VMEM_LIMIT_BYTES = 60 * 1024 * 1024

_BF16 = jnp.bfloat16
_F32 = jnp.float32
_U32 = jnp.uint32

_SMALL = ("pre_mix_g", "b_in", "pool_scale", "ln_g", "ln_b", "post_mix_g",
          "pre_ffn_g", "post_ffn_g", "post_ple_g")
_BIG = ("w_in", "pool_w", "w_pa", "w_pb", "w_o", "w_ff1", "w_ff2", "w_gate", "w_proj")


def _rms(x, g):
    return x * lax.rsqrt(jnp.mean(x * x, axis=-1, keepdims=True) + EPS) * g


def _dot(a, b):
    return jnp.dot(a, b, preferred_element_type=_F32)


def _wdot(a, w_packed):
    return _dot(a, pltpu.bitcast(w_packed, _BF16))


ACT_WEIGHT_SCALE = 0.5
_GELU_K1 = 2.0 * 0.7978845608028654
_GELU_K2 = 8.0 * 0.7978845608028654 * 0.044715


def _sigmoid_of_half(half_x):
    return 0.5 * jnp.tanh(half_x) + 0.5


def _gelu_tanh_of_half(half_x):
    return half_x * jnp.tanh(half_x * (_GELU_K1 + _GELU_K2 * (half_x * half_x))) + half_x


def _matrix_pieces(src, dst, halved_from_col=None):
    k, n = src.shape
    pieces = []
    for r in range(0, k, STAGE_ROWS):
        nr = min(STAGE_ROWS, k - r)
        for c in range(0, n, STAGE_COLS):
            nc = min(STAGE_COLS, n - c)
            halved = halved_from_col is not None and c >= halved_from_col
            assert halved_from_col is None or halved or c + nc <= halved_from_col
            pieces.append((src.at[pl.ds(r, nr), pl.ds(c, nc)],
                           dst.at[pl.ds(r // 2, nr // 2), pl.ds(c, nc)],
                           ACT_WEIGHT_SCALE if halved else None))
    return pieces


def _load_weights(pieces, stage_ref, sem_ref):
    slots = stage_ref.shape[0]

    def copy(i):
        src = pieces[i][0]
        nr, nc = src.shape
        return pltpu.make_async_copy(
            src, stage_ref.at[i % slots, pl.ds(0, nr), pl.ds(0, nc)], sem_ref.at[i % slots])

    for i in range(min(slots, len(pieces))):
        copy(i).start()
    for i, (src, dst, scale) in enumerate(pieces):
        nr, nc = src.shape
        copy(i).wait()
        piece = stage_ref[i % slots, 0:nr, 0:nc]
        if scale is not None:
            piece = piece * scale
        dst[...] = pltpu.bitcast(piece.astype(_BF16), _U32)
        if i + slots < len(pieces):
            copy(i + slots).start()


def _emit_interleaved(schedule):
    live = list(schedule)
    tick = 0
    while live:
        for item in list(live):
            prog, first, period = item
            if tick >= first and (tick - first) % period == 0:
                if next(prog, StopIteration) is StopIteration:
                    live.remove(item)
        tick += 1


def _window_sums(zext, w):
    s = zext
    k = 1
    while k < min(w, SUBLANES):
        s = s + pltpu.roll(s, k, axis=0)
        k *= 2
    win = s[HALO:, :]
    if w > SUBLANES:
        assert w == 2 * SUBLANES
        win = win + s[HALO - SUBLANES:-SUBLANES, :]
    return win


def _ffn_chunk(h, f, c, wt):
    a = _wdot(h, wt.w_ff1[:, c:c + FF_CHUNK])
    a = jnp.square(jnp.maximum(a, 0.0)).astype(_BF16)
    part = _wdot(a, wt.w_ff2[c // 2:(c + FF_CHUNK) // 2, :])
    return part if f is None else f + part


def _front_stages(row0, seq_row0, layer, x_ref, sm, w_s_ref, b_s_ref, wt, zbuf_ref):
    tm = SUB_TILE
    d = x_ref.shape[1]
    rows = slice(row0, row0 + tm)
    zrows = slice(HALO + row0, HALO + row0 + tm)
    lyr = slice(layer, layer + 1)

    x = x_ref[rows, :]
    h = _rms(x, sm.pre_mix_g[lyr, :]).astype(_BF16)
    yield

    def proj(lo, hi, scale=1.0):
        return _wdot(h, wt.w_in[:, lo:hi]) + scale * sm.b_in[lyr, lo:hi]

    def half_proj(lo, hi):
        return proj(lo, hi, ACT_WEIGHT_SCALE)

    def activated(act, lo, hi):
        mid = (lo + hi) // 2
        return jnp.concatenate([act(half_proj(lo, mid)), act(half_proj(mid, hi))], axis=1)

    zbuf_ref[zrows, :] = proj(0, d)
    yield
    v = activated(_gelu_tanh_of_half, 2 * d, 3 * d)
    yield

    t = seq_row0 + row0 + lax.broadcasted_iota(jnp.int32, (tm, LANES), 0)
    gdim = d // len(POOL_WINDOWS)
    pooled = []
    for k, w in enumerate(POOL_WINDOWS):
        zext = zbuf_ref[row0:row0 + HALO + tm, k * gdim:(k + 1) * gdim]
        win = _window_sums(zext, w)
        inv_cnt = 1.0 / jnp.minimum(t + 1, w).astype(_F32)
        inv_cnt = jnp.concatenate([inv_cnt] * (gdim // LANES), axis=1)
        pooled.append((win * inv_cnt - zext[HALO:, :]).astype(_BF16))
    yield
    gate_a = activated(_sigmoid_of_half, 3 * d, 4 * d)
    yield
    pm = jnp.concatenate([_wdot(pooled[k], wt.pool_w[k]) for k in range(len(POOL_WINDOWS))], axis=-1)
    pm = (pm * sm.pool_scale[lyr, :]).astype(_BF16)
    merged = gate_a * _wdot(pm, wt.w_pa[...])
    yield

    mu = jnp.mean(v, axis=-1, keepdims=True)
    vc = v - mu
    vn = (vc * lax.rsqrt(jnp.mean(vc * vc, axis=-1, keepdims=True) + EPS) * sm.ln_g[lyr, :]
          + sm.ln_b[lyr, :]).astype(_BF16)
    yield
    u = activated(_gelu_tanh_of_half, d, 2 * d)
    yield
    hd = d // SGU_GROUPS
    n_chunks = tm // SGU_CHUNK
    causal = (lax.broadcasted_iota(jnp.int32, (SGU_CHUNK, SGU_CHUNK), 0)
              >= lax.broadcasted_iota(jnp.int32, (SGU_CHUNK, SGU_CHUNK), 1))
    blocks = [[] for _ in range(n_chunks)]
    for g in range(SGU_GROUPS):
        ws = jnp.where(causal, w_s_ref[g], 0.0).astype(_BF16)
        rhs = jnp.concatenate([vn[c * SGU_CHUNK:(c + 1) * SGU_CHUNK, g * hd:(g + 1) * hd]
                               for c in range(n_chunks)], axis=1)
        out = _dot(ws, rhs)
        for c in range(n_chunks):
            blocks[c].append(out[:, c * hd:(c + 1) * hd])
    spatial = jnp.concatenate([jnp.concatenate(b, axis=1) + b_s_ref[...] for b in blocks], axis=0)
    yield
    gate_b = activated(_sigmoid_of_half, 4 * d, 5 * d)
    yield
    merged = merged + gate_b * _wdot((u * spatial).astype(_BF16), wt.w_pb[...])
    yield
    o = _wdot(merged.astype(_BF16), wt.w_o[...])
    yield
    x = x + _rms(o, sm.post_mix_g[lyr, :])
    h = _rms(x, sm.pre_ffn_g[lyr, :]).astype(_BF16)
    yield
    f = None
    for c in range(0, FRONT_FF_CHUNKS * FF_CHUNK, FF_CHUNK):
        f = _ffn_chunk(h, f, c, wt)
        yield
    return x, h, f


def _back_stages(x, h, f, p_bf16, layer, sm, wt):
    lyr = slice(layer, layer + 1)
    for c in range(FRONT_FF_CHUNKS * FF_CHUNK, wt.w_ff1.shape[1], FF_CHUNK):
        f = _ffn_chunk(h, f, c, wt)
        yield
    x = x + _rms(f, sm.post_ffn_g[lyr, :])
    yield
    gate = _sigmoid_of_half(_wdot(x.astype(_BF16), wt.w_gate[...]))
    e = _wdot(p_bf16, wt.w_proj[...])
    yield
    return x + _rms(gate * e, sm.post_ple_g[lyr, :])


def _layer_kernel(*refs, layer, tiles_per_seq, n_tiles):
    refs = list(refs)
    x_ref, p_ref, p_prev_ref = refs[:3]
    sm = SimpleNamespace(**dict(zip(_SMALL, refs[3:])))
    k = 3 + len(_SMALL)
    w_s_ref, b_s_ref = refs[k:k + 2]
    hbm = refs[k + 2:k + 2 + len(_BIG)]
    out_hbm = refs[k + 2 + len(_BIG)]
    packed = refs[k + 3 + len(_BIG):k + 3 + 2 * len(_BIG)]
    wt = SimpleNamespace(**dict(zip(_BIG, packed)))
    zbuf_ref, carry_x, carry_h, carry_f, stage_ref, osem_ref, wsem_ref = refs[k + 3 + 2 * len(_BIG):]

    tm, d = x_ref.shape
    assert stage_ref.shape == (STAGE_SLOTS, SUB_TILE, d) and STAGE_SLOTS == 4
    step = pl.program_id(0)
    j = step % tiles_per_seq
    par = step % 2

    def out_slot(parity, which):
        return 2 * parity + which

    def out_copy(parity, which, sub_tile):
        row = pl.multiple_of(sub_tile * SUB_TILE, SUB_TILE)
        slot = out_slot(parity, which)
        return pltpu.make_async_copy(stage_ref.at[slot], out_hbm.at[pl.ds(row, SUB_TILE), :],
                                     osem_ref.at[slot])

    @pl.when(step == 0)
    def _():
        pieces = []
        for name, src, dst in zip(_BIG, hbm, packed):
            if name == "pool_w":
                for g in range(len(POOL_WINDOWS)):
                    pieces += _matrix_pieces(src.at[layer, g], dst.at[g])
            else:
                halved_from = {"w_in": x_ref.shape[1], "w_gate": 0}.get(name)
                pieces += _matrix_pieces(src.at[layer], dst, halved_from)
        _load_weights(pieces, stage_ref, wsem_ref)
        carry_x[...] = jnp.zeros(carry_x.shape, carry_x.dtype)
        carry_h[...] = jnp.zeros(carry_h.shape, carry_h.dtype)
        carry_f[...] = jnp.zeros(carry_f.shape, carry_f.dtype)

    @pl.when(step >= 2)
    def _():
        out_copy(par, 1, 2 * (step - 2)).wait()

    @pl.when(step >= 3)
    def _():
        out_copy(par, 0, 2 * (step - 2) - 1).wait()

    def carried_program():
        state = (carry_x[...], carry_h[...], carry_f[...], p_prev_ref[...].astype(_BF16))
        stage_ref[out_slot(par, 0)] = yield from _back_stages(*state, layer, sm, wt)

    def lead_program():
        x, h, f = yield from _front_stages(0, j * tm, layer, x_ref, sm, w_s_ref, b_s_ref, wt, zbuf_ref)
        p_bf16 = p_ref[0:SUB_TILE, :].astype(_BF16)
        stage_ref[out_slot(par, 1)] = yield from _back_stages(x, h, f, p_bf16, layer, sm, wt)

    def trail_program():
        x, h, f = yield from _front_stages(SUB_TILE, j * tm, layer, x_ref, sm, w_s_ref, b_s_ref,
                                           wt, zbuf_ref)
        carry_x[...] = x
        carry_h[...] = h
        carry_f[...] = f

    @pl.when(step < n_tiles)
    def _():
        @pl.when(j == 0)
        def _():
            zbuf_ref[0:HALO, :] = jnp.zeros((HALO, d), _F32)

        @pl.when(j != 0)
        def _():
            zbuf_ref[0:HALO, :] = zbuf_ref[tm:tm + HALO, :]

        _emit_interleaved([(carried_program(), 0, 2), (lead_program(), 0, 1), (trail_program(), 4, 1)])
        out_copy(par, 1, 2 * step).start()

        @pl.when(step >= 1)
        def _():
            out_copy(par, 0, 2 * step - 1).start()

    @pl.when(step == n_tiles)
    def _():
        _emit_interleaved([(carried_program(), 0, 1)])
        out_copy(par, 0, 2 * step - 1).start()
        out_copy(par, 0, 2 * step - 1).wait()
        out_copy(1 - par, 1, 2 * (step - 1)).wait()
        out_copy(1 - par, 0, 2 * (step - 1) - 1).wait()


def _whole(a):
    nd = a.ndim
    return pl.BlockSpec(a.shape, lambda i: (0,) * nd, pipeline_mode=pl.Buffered(1))


def _of_layer(a, layer):
    nd = a.ndim
    return pl.BlockSpec((None,) + a.shape[1:], lambda i: (layer,) + (0,) * (nd - 1),
                        pipeline_mode=pl.Buffered(1))


def _layer_call(x, p, layer, seq, small, w_s, b_s_tile, big):
    n, d = x.shape
    tm = ROW_TILE
    n_tiles = n // tm
    assert n_tiles >= 2
    last = n_tiles - 1
    row = pl.BlockSpec((tm, d), lambda i: (jnp.minimum(i, last), 0))
    prow = pl.BlockSpec((None, tm, p.shape[2]), lambda i: (layer, jnp.minimum(i, last), 0))
    prow_prev = pl.BlockSpec((None, SUB_TILE, p.shape[2]),
                             lambda i: (layer, jnp.maximum(2 * i - 1, 0), 0))
    packed = [pltpu.VMEM(w.shape[1:-2] + (w.shape[-2] // 2, w.shape[-1]), _U32) for w in big]
    return pl.pallas_call(
        functools.partial(_layer_kernel, layer=layer, tiles_per_seq=seq // tm, n_tiles=n_tiles),
        out_shape=jax.ShapeDtypeStruct((n, d), _F32),
        grid=(n_tiles + 1,),
        in_specs=([row, prow, prow_prev] + [_whole(a) for a in small]
                  + [_of_layer(w_s, layer), _of_layer(b_s_tile, layer)]
                  + [pl.BlockSpec(memory_space=pl.ANY)] * len(big)),
        out_specs=pl.BlockSpec(memory_space=pl.ANY),
        scratch_shapes=packed + [
            pltpu.VMEM((HALO + tm, d), _F32),
            pltpu.VMEM((SUB_TILE, d), _F32),
            pltpu.VMEM((SUB_TILE, d), _BF16),
            pltpu.VMEM((SUB_TILE, d), _F32),
            pltpu.VMEM((STAGE_SLOTS, STAGE_ROWS, STAGE_COLS), _F32),
            pltpu.SemaphoreType.DMA((STAGE_SLOTS,)),
            pltpu.SemaphoreType.DMA((STAGE_SLOTS,))],
        compiler_params=pltpu.CompilerParams(
            dimension_semantics=("arbitrary",), vmem_limit_bytes=VMEM_LIMIT_BYTES),
        name="layer",
    )(x, p, p, *small, w_s, b_s_tile, *big)


def kernel(x, p, pre_mix_g, w_in, b_in, pool_w, pool_scale, sgu_ln_g, sgu_ln_b, sgu_w_s,
           sgu_b_s, w_pa, w_pb, w_o, post_mix_g, pre_ffn_g, w_ff1, w_ff2, post_ffn_g,
           w_ple_gate, w_ple_proj, post_ple_g):
    b, s, d = x.shape
    depth = w_in.shape[0]
    assert s % ROW_TILE == 0 and SUB_TILE % SGU_CHUNK == 0
    assert d % SGU_GROUPS == 0 and (d // len(POOL_WINDOWS)) % LANES == 0
    assert w_ff1.shape[2] % FF_CHUNK == 0
    xf = x.reshape(b * s, d)
    pf = p.reshape(depth, b * s, p.shape[-1])
    b_s_tile = jnp.repeat(jnp.swapaxes(sgu_b_s, 1, 2), d // SGU_GROUPS, axis=2)
    small = (pre_mix_g, b_in, pool_scale, sgu_ln_g, sgu_ln_b, post_mix_g,
             pre_ffn_g, post_ffn_g, post_ple_g)
    big = (w_in, pool_w, w_pa, w_pb, w_o, w_ff1, w_ff2, w_ple_gate, w_ple_proj)
    for i in range(depth):
        xf = _layer_call(xf, pf, i, s, small, sgu_w_s, b_s_tile, big)
    return xf.reshape(b, s, d)
```

```python
import functools
from types import SimpleNamespace

import jax
import jax.numpy as jnp
from jax import lax
from jax.experimental import pallas as pl
from jax.experimental.pallas import tpu as pltpu

POOL_WINDOWS = (2, 4, 8, 16)
SUBLANES, LANES = 8, 128
HALO = 16
SGU_CHUNK = 128
SGU_GROUPS = 8
EPS = 1e-6
FF_CHUNK = 1024
FRONT_FF_CHUNKS = 2
SUB_TILE = 256
ROW_TILE = 2 * SUB_TILE
STAGE_ROWS, STAGE_COLS = SUB_TILE, 1024
STAGE_SLOTS = 4
VMEM_LIMIT_BYTES = 60 * 1024 * 1024

_BF16 = jnp.bfloat16
_F32 = jnp.float32
_U32 = jnp.uint32

_SMALL = ("pre_mix_g", "b_in", "pool_scale", "ln_g", "ln_b", "post_mix_g",
          "pre_ffn_g", "post_ffn_g", "post_ple_g")
_BIG = ("w_in", "pool_w", "w_pa", "w_pb", "w_o", "w_ff1", "w_ff2", "w_gate", "w_proj")


def _rms(x, g):
    return x * lax.rsqrt(jnp.mean(x * x, axis=-1, keepdims=True) + EPS) * g


def _dot(a, b):
    return jnp.dot(a, b, preferred_element_type=_F32)


def _wdot(a, w_packed):
    return _dot(a, pltpu.bitcast(w_packed, _BF16))


ACT_WEIGHT_SCALE = 0.5
_GELU_K1 = 2.0 * 0.7978845608028654
_GELU_K2 = 8.0 * 0.7978845608028654 * 0.044715


def _sigmoid_of_half(half_x):
    return 0.5 * jnp.tanh(half_x) + 0.5


def _gelu_tanh_of_half(half_x):
    return half_x * jnp.tanh(half_x * (_GELU_K1 + _GELU_K2 * (half_x * half_x))) + half_x


def _matrix_pieces(src, dst, halved_from_col=None):
    k, n = src.shape
    pieces = []
    for r in range(0, k, STAGE_ROWS):
        nr = min(STAGE_ROWS, k - r)
        for c in range(0, n, STAGE_COLS):
            nc = min(STAGE_COLS, n - c)
            halved = halved_from_col is not None and c >= halved_from_col
            assert halved_from_col is None or halved or c + nc <= halved_from_col
            pieces.append((src.at[pl.ds(r, nr), pl.ds(c, nc)],
                           dst.at[pl.ds(r // 2, nr // 2), pl.ds(c, nc)],
                           ACT_WEIGHT_SCALE if halved else None))
    return pieces


def _load_weights(pieces, stage_ref, sem_ref):
    slots = stage_ref.shape[0]

    def copy(i):
        src = pieces[i][0]
        nr, nc = src.shape
        return pltpu.make_async_copy(
            src, stage_ref.at[i % slots, pl.ds(0, nr), pl.ds(0, nc)], sem_ref.at[i % slots])

    for i in range(min(slots, len(pieces))):
        copy(i).start()
    for i, (src, dst, scale) in enumerate(pieces):
        nr, nc = src.shape
        copy(i).wait()
        piece = stage_ref[i % slots, 0:nr, 0:nc]
        if scale is not None:
            piece = piece * scale
        dst[...] = pltpu.bitcast(piece.astype(_BF16), _U32)
        if i + slots < len(pieces):
            copy(i + slots).start()


def _emit_interleaved(schedule):
    live = list(schedule)
    tick = 0
    while live:
        for item in list(live):
            prog, first, period = item
            if tick >= first and (tick - first) % period == 0:
                if next(prog, StopIteration) is StopIteration:
                    live.remove(item)
        tick += 1


def _window_sums(zext, w):
    s = zext
    k = 1
    while k < min(w, SUBLANES):
        s = s + pltpu.roll(s, k, axis=0)
        k *= 2
    win = s[HALO:, :]
    if w > SUBLANES:
        assert w == 2 * SUBLANES
        win = win + s[HALO - SUBLANES:-SUBLANES, :]
    return win


def _ffn_chunk(h, f, c, wt):
    a = _wdot(h, wt.w_ff1[:, c:c + FF_CHUNK])
    a = jnp.square(jnp.maximum(a, 0.0)).astype(_BF16)
    part = _wdot(a, wt.w_ff2[c // 2:(c + FF_CHUNK) // 2, :])
    return part if f is None else f + part


def _front_stages(row0, seq_row0, layer, x_ref, sm, w_s_ref, b_s_ref, wt, zbuf_ref):
    tm = SUB_TILE
    d = x_ref.shape[1]
    rows = slice(row0, row0 + tm)
    zrows = slice(HALO + row0, HALO + row0 + tm)
    lyr = slice(layer, layer + 1)

    x = x_ref[rows, :]
    h = _rms(x, sm.pre_mix_g[lyr, :]).astype(_BF16)
    yield

    def proj(lo, hi, scale=1.0):
        return _wdot(h, wt.w_in[:, lo:hi]) + scale * sm.b_in[lyr, lo:hi]

    def half_proj(lo, hi):
        return proj(lo, hi, ACT_WEIGHT_SCALE)
    zbuf_ref[zrows, :] = proj(0, d)
    yield
    v = _gelu_tanh_of_half(half_proj(2 * d, 3 * d))
    yield

    t = seq_row0 + row0 + lax.broadcasted_iota(jnp.int32, (tm, LANES), 0)
    gdim = d // len(POOL_WINDOWS)
    pooled = []
    for k, w in enumerate(POOL_WINDOWS):
        zext = zbuf_ref[row0:row0 + HALO + tm, k * gdim:(k + 1) * gdim]
        win = _window_sums(zext, w)
        inv_cnt = 1.0 / jnp.minimum(t + 1, w).astype(_F32)
        inv_cnt = jnp.concatenate([inv_cnt] * (gdim // LANES), axis=1)
        pooled.append((win * inv_cnt - zext[HALO:, :]).astype(_BF16))
    yield
    gate_a = _sigmoid_of_half(half_proj(3 * d, 4 * d))
    yield
    pm = jnp.concatenate([_wdot(pooled[k], wt.pool_w[k]) for k in range(len(POOL_WINDOWS))], axis=-1)
    pm = (pm * sm.pool_scale[lyr, :]).astype(_BF16)
    merged = gate_a * _wdot(pm, wt.w_pa[...])
    yield

    mu = jnp.mean(v, axis=-1, keepdims=True)
    vc = v - mu
    vn = (vc * lax.rsqrt(jnp.mean(vc * vc, axis=-1, keepdims=True) + EPS) * sm.ln_g[lyr, :]
          + sm.ln_b[lyr, :]).astype(_BF16)
    yield
    u = _gelu_tanh_of_half(half_proj(d, 2 * d))
    yield
    hd = d // SGU_GROUPS
    n_chunks = tm // SGU_CHUNK
    causal = (lax.broadcasted_iota(jnp.int32, (SGU_CHUNK, SGU_CHUNK), 0)
              >= lax.broadcasted_iota(jnp.int32, (SGU_CHUNK, SGU_CHUNK), 1))
    blocks = [[] for _ in range(n_chunks)]
    for g in range(SGU_GROUPS):
        ws = jnp.where(causal, w_s_ref[g], 0.0).astype(_BF16)
        rhs = jnp.concatenate([vn[c * SGU_CHUNK:(c + 1) * SGU_CHUNK, g * hd:(g + 1) * hd]
                               for c in range(n_chunks)], axis=1)
        out = _dot(ws, rhs)
        for c in range(n_chunks):
            blocks[c].append(out[:, c * hd:(c + 1) * hd])
    spatial = jnp.concatenate([jnp.concatenate(b, axis=1) + b_s_ref[...] for b in blocks], axis=0)
    yield
    gate_b = _sigmoid_of_half(half_proj(4 * d, 5 * d))
    yield
    merged = merged + gate_b * _wdot((u * spatial).astype(_BF16), wt.w_pb[...])
    yield
    o = _wdot(merged.astype(_BF16), wt.w_o[...])
    yield
    x = x + _rms(o, sm.post_mix_g[lyr, :])
    h = _rms(x, sm.pre_ffn_g[lyr, :]).astype(_BF16)
    yield
    f = None
    for c in range(0, FRONT_FF_CHUNKS * FF_CHUNK, FF_CHUNK):
        f = _ffn_chunk(h, f, c, wt)
        yield
    return x, h, f


def _back_stages(get_x, get_h, get_f, get_p, layer, sm, wt):
    lyr = slice(layer, layer + 1)
    f = get_f()
    for c in range(FRONT_FF_CHUNKS * FF_CHUNK, wt.w_ff1.shape[1], FF_CHUNK):
        f = _ffn_chunk(get_h(), f, c, wt)
        yield
    x = get_x() + _rms(f, sm.post_ffn_g[lyr, :])
    yield
    gate = _sigmoid_of_half(_wdot(x.astype(_BF16), wt.w_gate[...]))
    e = _wdot(get_p(), wt.w_proj[...])
    yield
    return x + _rms(gate * e, sm.post_ple_g[lyr, :])


def _layer_kernel(*refs, layer, tiles_per_seq, n_tiles):
    refs = list(refs)
    x_ref, p_ref, p_prev_ref = refs[:3]
    sm = SimpleNamespace(**dict(zip(_SMALL, refs[3:])))
    k = 3 + len(_SMALL)
    w_s_ref, b_s_ref = refs[k:k + 2]
    hbm = refs[k + 2:k + 2 + len(_BIG)]
    out_hbm = refs[k + 2 + len(_BIG)]
    packed = refs[k + 3 + len(_BIG):k + 3 + 2 * len(_BIG)]
    wt = SimpleNamespace(**dict(zip(_BIG, packed)))
    zbuf_ref, carry_x, carry_h, carry_f, stage_ref, osem_ref, wsem_ref = refs[k + 3 + 2 * len(_BIG):]

    tm, d = x_ref.shape
    assert stage_ref.shape == (STAGE_SLOTS, SUB_TILE, d) and STAGE_SLOTS == 4
    step = pl.program_id(0)
    j = step % tiles_per_seq
    par = step % 2

    def out_slot(parity, which):
        return 2 * parity + which

    def out_copy(parity, which, sub_tile):
        row = pl.multiple_of(sub_tile * SUB_TILE, SUB_TILE)
        slot = out_slot(parity, which)
        return pltpu.make_async_copy(stage_ref.at[slot], out_hbm.at[pl.ds(row, SUB_TILE), :],
                                     osem_ref.at[slot])

    @pl.when(step == 0)
    def _():
        pieces = []
        for name, src, dst in zip(_BIG, hbm, packed):
            if name == "pool_w":
                for g in range(len(POOL_WINDOWS)):
                    pieces += _matrix_pieces(src.at[layer, g], dst.at[g])
            else:
                halved_from = {"w_in": x_ref.shape[1], "w_gate": 0}.get(name)
                pieces += _matrix_pieces(src.at[layer], dst, halved_from)
        _load_weights(pieces, stage_ref, wsem_ref)
        carry_x[...] = jnp.zeros(carry_x.shape, carry_x.dtype)
        carry_h[...] = jnp.zeros(carry_h.shape, carry_h.dtype)
        carry_f[...] = jnp.zeros(carry_f.shape, carry_f.dtype)

    @pl.when(step >= 2)
    def _():
        out_copy(par, 1, 2 * (step - 2)).wait()

    @pl.when(step >= 3)
    def _():
        out_copy(par, 0, 2 * (step - 2) - 1).wait()

    def carried_program():
        stage_ref[out_slot(par, 0)] = yield from _back_stages(
            lambda: carry_x[...], lambda: carry_h[...], lambda: carry_f[...],
            lambda: p_prev_ref[...].astype(_BF16), layer, sm, wt)

    def lead_program():
        x, h, f = yield from _front_stages(0, j * tm, layer, x_ref, sm, w_s_ref, b_s_ref, wt, zbuf_ref)
        stage_ref[out_slot(par, 1)] = yield from _back_stages(
            lambda: x, lambda: h, lambda: f, lambda: p_ref[0:SUB_TILE, :].astype(_BF16), layer, sm, wt)

    def trail_program():
        x, h, f = yield from _front_stages(SUB_TILE, j * tm, layer, x_ref, sm, w_s_ref, b_s_ref,
                                           wt, zbuf_ref)
        carry_x[...] = x
        carry_h[...] = h
        carry_f[...] = f

    @pl.when(step < n_tiles)
    def _():
        @pl.when(j == 0)
        def _():
            zbuf_ref[0:HALO, :] = jnp.zeros((HALO, d), _F32)

        @pl.when(j != 0)
        def _():
            zbuf_ref[0:HALO, :] = zbuf_ref[tm:tm + HALO, :]

        _emit_interleaved([(carried_program(), 0, 2), (lead_program(), 0, 1), (trail_program(), 4, 1)])
        out_copy(par, 1, 2 * step).start()

        @pl.when(step >= 1)
        def _():
            out_copy(par, 0, 2 * step - 1).start()

    @pl.when(step == n_tiles)
    def _():
        _emit_interleaved([(carried_program(), 0, 1)])
        out_copy(par, 0, 2 * step - 1).start()
        out_copy(par, 0, 2 * step - 1).wait()
        out_copy(1 - par, 1, 2 * (step - 1)).wait()
        out_copy(1 - par, 0, 2 * (step - 1) - 1).wait()


def _whole(a):
    nd = a.ndim
    return pl.BlockSpec(a.shape, lambda i: (0,) * nd, pipeline_mode=pl.Buffered(1))


def _of_layer(a, layer):
    nd = a.ndim
    return pl.BlockSpec((None,) + a.shape[1:], lambda i: (layer,) + (0,) * (nd - 1),
                        pipeline_mode=pl.Buffered(1))


def _layer_call(x, p, layer, seq, small, w_s, b_s_tile, big):
    n, d = x.shape
    tm = ROW_TILE
    n_tiles = n // tm
    assert n_tiles >= 2
    last = n_tiles - 1
    row = pl.BlockSpec((tm, d), lambda i: (jnp.minimum(i, last), 0))
    prow = pl.BlockSpec((None, tm, p.shape[2]), lambda i: (layer, jnp.minimum(i, last), 0))
    prow_prev = pl.BlockSpec((None, SUB_TILE, p.shape[2]),
                             lambda i: (layer, jnp.maximum(2 * i - 1, 0), 0))
    packed = [pltpu.VMEM(w.shape[1:-2] + (w.shape[-2] // 2, w.shape[-1]), _U32) for w in big]
    return pl.pallas_call(
        functools.partial(_layer_kernel, layer=layer, tiles_per_seq=seq // tm, n_tiles=n_tiles),
        out_shape=jax.ShapeDtypeStruct((n, d), _F32),
        grid=(n_tiles + 1,),
        in_specs=([row, prow, prow_prev] + [_whole(a) for a in small]
                  + [_of_layer(w_s, layer), _of_layer(b_s_tile, layer)]
                  + [pl.BlockSpec(memory_space=pl.ANY)] * len(big)),
        out_specs=pl.BlockSpec(memory_space=pl.ANY),
        scratch_shapes=packed + [
            pltpu.VMEM((HALO + tm, d), _F32),
            pltpu.VMEM((SUB_TILE, d), _F32),
            pltpu.VMEM((SUB_TILE, d), _BF16),
            pltpu.VMEM((SUB_TILE, d), _F32),
            pltpu.VMEM((STAGE_SLOTS, STAGE_ROWS, STAGE_COLS), _F32),
            pltpu.SemaphoreType.DMA((STAGE_SLOTS,)),
            pltpu.SemaphoreType.DMA((STAGE_SLOTS,))],
        compiler_params=pltpu.CompilerParams(
            dimension_semantics=("arbitrary",), vmem_limit_bytes=VMEM_LIMIT_BYTES),
        name="layer",
    )(x, p, p, *small, w_s, b_s_tile, *big)


def kernel(x, p, pre_mix_g, w_in, b_in, pool_w, pool_scale, sgu_ln_g, sgu_ln_b, sgu_w_s,
           sgu_b_s, w_pa, w_pb, w_o, post_mix_g, pre_ffn_g, w_ff1, w_ff2, post_ffn_g,
           w_ple_gate, w_ple_proj, post_ple_g):
    b, s, d = x.shape
    depth = w_in.shape[0]
    assert s % ROW_TILE == 0 and SUB_TILE % SGU_CHUNK == 0
    assert d % SGU_GROUPS == 0 and (d // len(POOL_WINDOWS)) % LANES == 0
    assert w_ff1.shape[2] % FF_CHUNK == 0
    xf = x.reshape(b * s, d)
    pf = p.reshape(depth, b * s, p.shape[-1])
    b_s_tile = jnp.repeat(jnp.swapaxes(sgu_b_s, 1, 2), d // SGU_GROUPS, axis=2)
    small = (pre_mix_g, b_in, pool_scale, sgu_ln_g, sgu_ln_b, post_mix_g,
             pre_ffn_g, post_ffn_g, post_ple_g)
    big = (w_in, pool_w, w_pa, w_pb, w_o, w_ff1, w_ff2, w_ple_gate, w_ple_proj)
    for i in range(depth):
        xf = _layer_call(xf, pf, i, s, small, sgu_w_s, b_s_tile, big)
    return xf.reshape(b, s, d)
```
